```python
import math
import jax, jax.numpy as jnp
from jax import lax
import numpy as np


D_MODEL = 2048
BATCH = 4
SEQ = 4096
DEPTH = 2
DEC_BATCH = 2
DEC_SEQ = 4096
PAST_LEN = 128

GRID_W = 64
NA_HEADS = 12
NA_HEAD_DIM = 64
NA_WIDTH = NA_HEADS * NA_HEAD_DIM
NA_WIN_ROWS_MAX = 8
NA_WIN_COLS = 16
RPB_ROWS = 2 * NA_WIN_ROWS_MAX - 1
RPB_COLS = 2 * NA_WIN_COLS - 1
CONV_WIDTH = 768
CONV_KERNEL = 31
MEM_TOKENS = 256
MEM_HEADS = 4
MEM_HEAD_DIM = 128
MEM_WIDTH = MEM_HEADS * MEM_HEAD_DIM
D_FF = -(-8 * D_MODEL // (3 * 256)) * 256
DEEPNORM_ALPHA = (2 * DEPTH) ** 0.25
DEEPNORM_BETA = (8 * DEPTH) ** -0.25
LN_EPS = 1e-5
IN_SPLITS = (
    NA_WIDTH,
    2 * NA_WIDTH,
    3 * NA_WIDTH,
    3 * NA_WIDTH + 2 * CONV_WIDTH,
    3 * NA_WIDTH + 2 * CONV_WIDTH + MEM_WIDTH,
    3 * NA_WIDTH + 2 * CONV_WIDTH + MEM_WIDTH + D_MODEL,
    3 * NA_WIDTH + 2 * CONV_WIDTH + MEM_WIDTH + 2 * D_MODEL,
)
IN_TOTAL = 3 * NA_WIDTH + 2 * CONV_WIDTH + MEM_WIDTH + 3 * D_MODEL

kernel_name = "hybrid_natten_conformer_memory_encoder"


def layer_norm(x, g, b):
    xf = x.astype(jnp.float32)
    mu = jnp.mean(xf, axis=-1, keepdims=True)
    var = jnp.mean(jnp.square(xf - mu), axis=-1, keepdims=True)
    y = (xf - mu) * lax.rsqrt(var + LN_EPS)
    return (y * g.astype(jnp.float32) + b.astype(jnp.float32)).astype(x.dtype)


def neighbourhood_attention(q, k, v, rpb):
    B, T, H, dh = q.shape
    rows = T // GRID_W
    kh = min(NA_WIN_ROWS_MAX, rows)
    kw = NA_WIN_COLS
    qg = (q * (dh ** -0.5)).reshape(B, rows, GRID_W, H, dh).transpose(1, 0, 2, 3, 4)
    kg = k.reshape(B, rows, GRID_W, H, dh)
    vg = v.reshape(B, rows, GRID_W, H, dh)
    cols = np.arange(GRID_W)
    col_start = np.clip(cols - kw // 2, 0, GRID_W - kw)
    col_idx = col_start[:, None] + np.arange(kw)[None, :]
    col_off = col_idx - cols[:, None] + (NA_WIN_COLS - 1)
    col_bias = rpb[:, :, col_off].astype(jnp.float32)

    def row_block(args):
        r, q_r = args
        start = jnp.clip(r - kh // 2, 0, rows - kh)
        k_rows = lax.dynamic_slice_in_dim(kg, start, kh, axis=1)
        v_rows = lax.dynamic_slice_in_dim(vg, start, kh, axis=1)
        k_win = k_rows[:, :, col_idx]
        v_win = v_rows[:, :, col_idx]
        row_off = start + jnp.arange(kh) - r + (NA_WIN_ROWS_MAX - 1)
        bias = jnp.take(col_bias, row_off, axis=1).transpose(0, 2, 1, 3)
        s = jnp.einsum('bqhd,biqjhd->bhqij', q_r, k_win).astype(jnp.float32) + bias[None]
        p = jax.nn.softmax(s.reshape(B, H, GRID_W, kh * kw), axis=-1)
        p = p.reshape(B, H, GRID_W, kh, kw).astype(v.dtype)
        return jnp.einsum('bhqij,biqjhd->bqhd', p, v_win)

    out = lax.map(row_block, (jnp.arange(rows), qg))
    return out.transpose(1, 0, 2, 3, 4).reshape(B, T, H * dh)


def conformer_conv(u_in, conv_w, conv_b, ln_g, ln_b):
    a, g = jnp.split(u_in, 2, axis=-1)
    u = a * jax.nn.sigmoid(g)
    pad = CONV_KERNEL // 2
    u = lax.conv_general_dilated(
        u, conv_w[:, None, :], window_strides=(1,), padding=((pad, pad),),
        dimension_numbers=('NWC', 'WIO', 'NWC'), feature_group_count=CONV_WIDTH) + conv_b
    return jax.nn.silu(layer_norm(u, ln_g, ln_b))


def memory_attention(q, mem_kv):
    B, T, _ = q.shape
    M = mem_kv.shape[1]
    qh = (q * (MEM_HEAD_DIM ** -0.5)).reshape(B, T, MEM_HEADS, MEM_HEAD_DIM)
    k, v = jnp.split(mem_kv, 2, axis=-1)
    k = k.reshape(B, M, MEM_HEADS, MEM_HEAD_DIM)
    v = v.reshape(B, M, MEM_HEADS, MEM_HEAD_DIM)
    s = jnp.einsum('bthd,bmhd->bhtm', qh, k).astype(jnp.float32)
    p = jax.nn.softmax(s, axis=-1).astype(v.dtype)
    return jnp.einsum('bhtm,bmhd->bthd', p, v).reshape(B, T, MEM_WIDTH)


def encoder_layer(x, mem, w_in, w_mem_kv, rpb, conv_w, conv_b, conv_ln_g, conv_ln_b,
                  w_pa, w_pb, w_pc, w_o, ln1_g, ln1_b, w_ffn_in, w_ffn_out, ln2_g, ln2_b):
    B, T, _ = x.shape
    h = x @ w_in
    q_na, k_na, v_na, u_conv, q_mem, g_na, g_conv, g_mem = jnp.split(h, IN_SPLITS, axis=-1)
    shp = (B, T, NA_HEADS, NA_HEAD_DIM)
    y_na = neighbourhood_attention(q_na.reshape(shp), k_na.reshape(shp), v_na.reshape(shp), rpb) @ w_pa
    y_conv = conformer_conv(u_conv, conv_w, conv_b, conv_ln_g, conv_ln_b) @ w_pb
    y_mem = memory_attention(q_mem, mem @ w_mem_kv) @ w_pc
    mixed = (jax.nn.sigmoid(g_na) * y_na + jax.nn.sigmoid(g_conv) * y_conv
             + jax.nn.sigmoid(g_mem) * y_mem)
    x = layer_norm(DEEPNORM_ALPHA * x + mixed @ w_o, ln1_g, ln1_b)
    gate, up = jnp.split(x @ w_ffn_in, 2, axis=-1)
    x = layer_norm(DEEPNORM_ALPHA * x + (jax.nn.silu(gate) * up) @ w_ffn_out, ln2_g, ln2_b)
    return x


def run_trunk(x, mem, params):
    for l in range(DEPTH):
        x = encoder_layer(x, mem, *[p[l] for p in params])
    return x


def setup_inputs(seed: int = 0) -> dict:
    key = jax.random.key(seed)
    ks = jax.random.split(key, 24)
    f32 = jnp.float32
    nrm = lambda k, shape, s: jax.random.normal(k, shape, f32) * s
    return {
        "x_prompt": nrm(ks[0], (BATCH, SEQ, D_MODEL), 1.0),
        "x_sample": nrm(ks[1], (DEC_BATCH, DEC_SEQ, D_MODEL), 1.0),
        "mem_prompt": nrm(ks[2], (BATCH, MEM_TOKENS, D_MODEL), 1.0),
        "mem_sample": nrm(ks[3], (DEC_BATCH, MEM_TOKENS, D_MODEL), 1.0),
        "w_in": nrm(ks[4], (DEPTH, D_MODEL, IN_TOTAL), D_MODEL ** -0.5),
        "w_mem_kv": nrm(ks[5], (DEPTH, D_MODEL, 2 * MEM_WIDTH), D_MODEL ** -0.5),
        "rpb": nrm(ks[6], (DEPTH, NA_HEADS, RPB_ROWS, RPB_COLS), 0.1),
        "conv_w": nrm(ks[7], (DEPTH, CONV_KERNEL, CONV_WIDTH), CONV_KERNEL ** -0.5),
        "conv_b": nrm(ks[8], (DEPTH, CONV_WIDTH), 0.01),
        "conv_ln_g": 1.0 + nrm(ks[9], (DEPTH, CONV_WIDTH), 0.01),
        "conv_ln_b": nrm(ks[10], (DEPTH, CONV_WIDTH), 0.01),
        "w_pa": nrm(ks[11], (DEPTH, NA_WIDTH, D_MODEL), NA_WIDTH ** -0.5),
        "w_pb": nrm(ks[12], (DEPTH, CONV_WIDTH, D_MODEL), CONV_WIDTH ** -0.5),
        "w_pc": nrm(ks[13], (DEPTH, MEM_WIDTH, D_MODEL), MEM_WIDTH ** -0.5),
        "w_o": nrm(ks[14], (DEPTH, D_MODEL, D_MODEL), D_MODEL ** -0.5 * DEEPNORM_BETA),
        "ln1_g": 1.0 + nrm(ks[15], (DEPTH, D_MODEL), 0.01),
        "ln1_b": nrm(ks[16], (DEPTH, D_MODEL), 0.01),
        "w_ffn_in": nrm(ks[17], (DEPTH, D_MODEL, 2 * D_FF), D_MODEL ** -0.5),
        "w_ffn_out": nrm(ks[18], (DEPTH, D_FF, D_MODEL), D_FF ** -0.5 * DEEPNORM_BETA),
        "ln2_g": 1.0 + nrm(ks[19], (DEPTH, D_MODEL), 0.01),
        "ln2_b": nrm(ks[20], (DEPTH, D_MODEL), 0.01),
    }


def reference(x_prompt, x_sample, mem_prompt, mem_sample, w_in, w_mem_kv, rpb, conv_w, conv_b,
              conv_ln_g, conv_ln_b, w_pa, w_pb, w_pc, w_o, ln1_g, ln1_b, w_ffn_in, w_ffn_out,
              ln2_g, ln2_b):
    params = (w_in, w_mem_kv, rpb, conv_w, conv_b, conv_ln_g, conv_ln_b, w_pa, w_pb, w_pc, w_o,
              ln1_g, ln1_b, w_ffn_in, w_ffn_out, ln2_g, ln2_b)
    y_prompt = run_trunk(x_prompt, mem_prompt, params)
    y_sample = run_trunk(x_sample, mem_sample, params)
    return (y_prompt, y_sample)
```

```python
import functools

import jax
import jax.numpy as jnp
import numpy as np
from jax import lax
from jax.experimental import pallas as pl
from jax.experimental.pallas import tpu as pltpu

F32 = jnp.float32
BF16 = jnp.bfloat16

GRID_W = 64
NA_HEADS = 12
NA_HEAD_DIM = 64
NA_WIDTH = NA_HEADS * NA_HEAD_DIM
NA_WIN_ROWS = 8
NA_WIN_COLS = 16
RPB_ROWS = 2 * NA_WIN_ROWS - 1
RPB_COLS = 2 * NA_WIN_COLS - 1
CONV_WIDTH = 768
CONV_KERNEL = 31
MEM_HEADS = 4
MEM_HEAD_DIM = 128
MEM_WIDTH = MEM_HEADS * MEM_HEAD_DIM
LN_EPS = 1e-5
MASK_VALUE = -1e30

V7X_LANES = 128
V7X_VMEM_BYTES = 64 * 1024 * 1024
V7X_VMEM_RESERVE_BYTES = 8 * 1024 * 1024
KERNEL_TEMP_BYTES = 16 * 1024 * 1024

IN_PROJ_TM = 1024
IN_PROJ_TN = 512
NA_ROWS_PER_STEP = 8
CONV_TT = 256
CONV_HALO = 16
CONV_ROW_CHUNK = 32
MEM_TQ = 1024
MIX_TM = 512
MIX_CHUNK = 512
PROJ_TM = 512
FFN_TM = 512
FFN_TF = 512


def _params(semantics, block_bytes):
    limit = min(block_bytes + KERNEL_TEMP_BYTES, V7X_VMEM_BYTES - V7X_VMEM_RESERVE_BYTES)
    return pltpu.CompilerParams(dimension_semantics=semantics, vmem_limit_bytes=int(limit))


def _sigmoid(x):
    return 1.0 / (1.0 + jnp.exp(-x))


def _layer_norm(z, g, b):
    mu = jnp.mean(z, axis=-1, keepdims=True)
    zc = z - mu
    var = jnp.mean(zc * zc, axis=-1, keepdims=True)
    return zc * lax.rsqrt(var + LN_EPS) * g + b


def _matmul_kernel(x_ref, w_ref, o_ref, xb_ref):
    @pl.when(pl.program_id(1) == 0)
    def _cast():
        xb_ref[...] = x_ref[...].astype(BF16)

    o_ref[...] = jnp.dot(xb_ref[...], w_ref[...], preferred_element_type=F32).astype(o_ref.dtype)


def _matmul(x, w, *, tm, tn, name):
    m, k = x.shape
    n = w.shape[1]
    tm = min(tm, m)
    assert m % tm == 0 and n % tn == 0
    vmem = 2 * tm * k * 4 + tm * k * 2 + 2 * k * tn * 2 + 2 * tm * tn * 2
    return pl.pallas_call(
        _matmul_kernel,
        grid=(m // tm, n // tn),
        in_specs=[pl.BlockSpec((tm, k), lambda i, j: (i, 0)),
                  pl.BlockSpec((k, tn), lambda i, j: (0, j))],
        out_specs=pl.BlockSpec((tm, tn), lambda i, j: (i, j)),
        out_shape=jax.ShapeDtypeStruct((m, n), BF16),
        scratch_shapes=[pltpu.VMEM((tm, k), BF16)],
        compiler_params=_params(("parallel", "arbitrary"), vmem),
        name=name,
    )(x, w)


def _na_bias_table(rpb):
    heads = rpb.shape[0]
    d = np.arange(NA_WIN_ROWS)[:, None]
    i = np.arange(NA_WIN_ROWS)[None, :]
    row_off = i - d + (NA_WIN_ROWS - 1)
    q = np.arange(GRID_W)[:, None]
    c = np.arange(GRID_W)[None, :]
    col_start = np.clip(q - NA_WIN_COLS // 2, 0, GRID_W - NA_WIN_COLS)
    valid = (c >= col_start) & (c < col_start + NA_WIN_COLS)
    col_off = np.clip(c - q + (NA_WIN_COLS - 1), 0, RPB_COLS - 1)
    t = rpb.astype(F32)[:, row_off[:, None, :, None], col_off[None, :, None, :]]
    t = jnp.where(valid[None, None, :, None, :], t, MASK_VALUE)
    t = t.reshape(heads // 2, 2, NA_WIN_ROWS, GRID_W, NA_WIN_ROWS * GRID_W)
    return t.transpose(0, 2, 1, 3, 4).reshape(heads // 2, NA_WIN_ROWS, 2 * GRID_W, NA_WIN_ROWS * GRID_W)


def _na_kernel(q_ref, k_ref, v_ref, bias_ref, o_ref, *, n_rows):
    rb = pl.program_id(2)
    win = NA_WIN_ROWS * GRID_W
    lane = lax.broadcasted_iota(jnp.int32, (GRID_W, 2 * NA_HEAD_DIM), 1)
    first_head = lane < NA_HEAD_DIM
    scale = NA_HEAD_DIM ** -0.5

    def row(i, carry):
        r = rb * NA_ROWS_PER_STEP + i
        start = jnp.clip(r - NA_WIN_ROWS // 2, 0, n_rows - NA_WIN_ROWS)
        shift = r - start
        q = q_ref[0, pl.ds(pl.multiple_of(i * GRID_W, GRID_W), GRID_W), :]
        kw = k_ref[0, pl.ds(pl.multiple_of(start * GRID_W, GRID_W), win), :]
        vw = v_ref[0, pl.ds(pl.multiple_of(start * GRID_W, GRID_W), win), :]
        zero = jnp.zeros_like(q)
        q2 = jnp.concatenate([jnp.where(first_head, q, zero), jnp.where(first_head, zero, q)], axis=0)
        s = lax.dot_general(q2, kw, (((1,), (1,)), ((), ())), preferred_element_type=F32)
        s = s * scale + bias_ref[0, shift]
        m = jnp.max(s, axis=-1, keepdims=True)
        p = jnp.exp(s - m)
        l = jnp.sum(p, axis=-1, keepdims=True)
        o2 = jnp.dot(p.astype(BF16), vw, preferred_element_type=F32) / l
        o = jnp.where(first_head, o2[:GRID_W], o2[GRID_W:])
        o_ref[0, pl.ds(pl.multiple_of(i * GRID_W, GRID_W), GRID_W), :] = o.astype(o_ref.dtype)
        return carry

    lax.fori_loop(0, NA_ROWS_PER_STEP, row, 0)


def _na_attention(h3, bias, *, q_col, k_col, v_col):
    b, t, _ = h3.shape
    n_rows = t // GRID_W
    pairs = NA_HEADS // 2
    tq = NA_ROWS_PER_STEP * GRID_W
    assert n_rows % NA_ROWS_PER_STEP == 0 and n_rows >= NA_WIN_ROWS
    lanes = 2 * NA_HEAD_DIM
    bias_bytes = NA_WIN_ROWS * 2 * GRID_W * NA_WIN_ROWS * GRID_W * 4
    vmem = 2 * (2 * tq * lanes * 2 + 2 * t * lanes * 2 + bias_bytes)
    return pl.pallas_call(
        functools.partial(_na_kernel, n_rows=n_rows),
        grid=(b, pairs, n_rows // NA_ROWS_PER_STEP),
        in_specs=[
            pl.BlockSpec((1, tq, lanes), lambda bi, hp, rb: (bi, rb, q_col + hp)),
            pl.BlockSpec((1, t, lanes), lambda bi, hp, rb: (bi, 0, k_col + hp)),
            pl.BlockSpec((1, t, lanes), lambda bi, hp, rb: (bi, 0, v_col + hp)),
            pl.BlockSpec((1, NA_WIN_ROWS, 2 * GRID_W, NA_WIN_ROWS * GRID_W), lambda bi, hp, rb: (hp, 0, 0, 0)),
        ],
        out_specs=pl.BlockSpec((1, tq, lanes), lambda bi, hp, rb: (bi, rb, hp)),
        out_shape=jax.ShapeDtypeStruct((b, t, NA_WIDTH), BF16),
        compiler_params=_params(("parallel", "parallel", "arbitrary"), vmem),
        name="na_attn",
    )(h3, h3, h3, bias)


def _conv_kernel(a_ref, g_ref, ap_ref, gp_ref, an_ref, gn_ref, w_ref, cb_ref, lg_ref, lb_ref, o_ref, u_ref,
                 *, n_tiles):
    ti = pl.program_id(1)
    tt = a_ref.shape[1]
    pad = CONV_KERNEL // 2

    def glu(a, g):
        return a.astype(F32) * _sigmoid(g.astype(F32))

    u_ref[CONV_HALO:CONV_HALO + tt, :] = glu(a_ref[0], g_ref[0])
    u_ref[0:CONV_HALO, :] = jnp.where(ti > 0, glu(ap_ref[0], gp_ref[0]), 0.0)
    u_ref[CONV_HALO + tt:2 * CONV_HALO + tt, :] = jnp.where(ti < n_tiles - 1, glu(an_ref[0], gn_ref[0]), 0.0)

    for c in range(tt // CONV_ROW_CHUNK):
        base = c * CONV_ROW_CHUNK + CONV_HALO - pad
        acc = jnp.zeros((CONV_ROW_CHUNK, CONV_WIDTH), F32)
        for j in range(CONV_KERNEL):
            acc = acc + u_ref[base + j:base + j + CONV_ROW_CHUNK, :] * w_ref[j:j + 1, :]
        y = _layer_norm(acc + cb_ref[...], lg_ref[...], lb_ref[...])
        y = y * _sigmoid(y)
        o_ref[0, c * CONV_ROW_CHUNK:(c + 1) * CONV_ROW_CHUNK, :] = y.astype(o_ref.dtype)


def _conv_module(h3, conv_w, conv_b, ln_g, ln_b, *, a_col, g_col):
    b, t, _ = h3.shape
    tt = min(CONV_TT, t)
    assert t % tt == 0 and tt % CONV_HALO == 0 and tt % CONV_ROW_CHUNK == 0
    n_tiles = t // tt
    hpt = tt // CONV_HALO
    last_halo = t // CONV_HALO - 1
    cw = CONV_WIDTH

    def cur(col):
        return pl.BlockSpec((1, tt, cw), lambda bi, ti: (bi, ti, col))

    def prev(col):
        return pl.BlockSpec((1, CONV_HALO, cw), lambda bi, ti: (bi, jnp.maximum(ti * hpt - 1, 0), col))

    def nxt(col):
        return pl.BlockSpec((1, CONV_HALO, cw), lambda bi, ti: (bi, jnp.minimum((ti + 1) * hpt, last_halo), col))

    def vec(rows):
        return pl.BlockSpec((rows, cw), lambda bi, ti: (0, 0))

    vmem = 2 * (2 * tt * cw * 2 + 4 * CONV_HALO * cw * 2 + tt * cw * 2) + (tt + 2 * CONV_HALO) * cw * 4 \
        + 2 * (CONV_KERNEL + 3) * cw * 4
    return pl.pallas_call(
        functools.partial(_conv_kernel, n_tiles=n_tiles),
        grid=(b, n_tiles),
        in_specs=[cur(a_col), cur(g_col), prev(a_col), prev(g_col), nxt(a_col), nxt(g_col),
                  vec(CONV_KERNEL), vec(1), vec(1), vec(1)],
        out_specs=pl.BlockSpec((1, tt, cw), lambda bi, ti: (bi, ti, 0)),
        out_shape=jax.ShapeDtypeStruct((b, t, cw), BF16),
        scratch_shapes=[pltpu.VMEM((tt + 2 * CONV_HALO, cw), F32)],
        compiler_params=_params(("parallel", "parallel"), vmem),
        name="conv_mod",
    )(h3, h3, h3, h3, h3, h3, conv_w, conv_b.reshape(1, cw), ln_g.reshape(1, cw), ln_b.reshape(1, cw))


def _mem_attn_kernel(q_ref, k_ref, v_ref, o_ref):
    s = lax.dot_general(q_ref[0], k_ref[0], (((1,), (1,)), ((), ())), preferred_element_type=F32)
    s = s * (MEM_HEAD_DIM ** -0.5)
    m = jnp.max(s, axis=-1, keepdims=True)
    p = jnp.exp(s - m)
    l = jnp.sum(p, axis=-1, keepdims=True)
    o = jnp.dot(p.astype(BF16), v_ref[0], preferred_element_type=F32) / l
    o_ref[0] = o.astype(o_ref.dtype)


def _mem_attention(h3, kv3, *, q_col):
    b, t, _ = h3.shape
    mt = kv3.shape[1]
    tq = min(MEM_TQ, t)
    assert t % tq == 0
    hd = MEM_HEAD_DIM
    vmem = 2 * (2 * tq * hd * 2 + 2 * mt * hd * 2) + 3 * tq * mt * 4
    return pl.pallas_call(
        _mem_attn_kernel,
        grid=(b, t // tq, MEM_HEADS),
        in_specs=[pl.BlockSpec((1, tq, hd), lambda bi, ti, hi: (bi, ti, q_col + hi)),
                  pl.BlockSpec((1, mt, hd), lambda bi, ti, hi: (bi, 0, hi)),
                  pl.BlockSpec((1, mt, hd), lambda bi, ti, hi: (bi, 0, MEM_HEADS + hi))],
        out_specs=pl.BlockSpec((1, tq, hd), lambda bi, ti, hi: (bi, ti, hi)),
        out_shape=jax.ShapeDtypeStruct((b, t, MEM_WIDTH), BF16),
        compiler_params=_params(("parallel", "parallel", "parallel"), vmem),
        name="mem_attn",
    )(h3, kv3, kv3)


def _mix_kernel(yna_ref, yconv_ref, ymem_ref, gna_ref, gconv_ref, gmem_ref, wpa_ref, wpb_ref, wpc_ref, o_ref):
    d = o_ref.shape[1]
    for n in range(d // MIX_CHUNK):
        sl = slice(n * MIX_CHUNK, (n + 1) * MIX_CHUNK)
        ya = jnp.dot(yna_ref[...], wpa_ref[:, sl], preferred_element_type=F32)
        yb = jnp.dot(yconv_ref[...], wpb_ref[:, sl], preferred_element_type=F32)
        yc = jnp.dot(ymem_ref[...], wpc_ref[:, sl], preferred_element_type=F32)
        mixed = (_sigmoid(gna_ref[:, sl].astype(F32)) * ya
                 + _sigmoid(gconv_ref[:, sl].astype(F32)) * yb
                 + _sigmoid(gmem_ref[:, sl].astype(F32)) * yc)
        o_ref[:, sl] = mixed.astype(o_ref.dtype)


def _mix(y_na, y_conv, y_mem, h, w_pa, w_pb, w_pc, *, gate_col):
    m = y_na.shape[0]
    d = w_pa.shape[1]
    tm = min(MIX_TM, m)
    assert m % tm == 0 and d % MIX_CHUNK == 0

    def rows(width, col=0):
        return pl.BlockSpec((tm, width), lambda i: (i, col))

    def whole(w):
        return pl.BlockSpec(w.shape, lambda i: (0, 0), pipeline_mode=pl.Buffered(1))

    vmem = 2 * tm * (2 * NA_WIDTH + MEM_WIDTH + 3 * d + d) * 2 + (2 * NA_WIDTH + MEM_WIDTH) * d * 2 \
        + 6 * tm * MIX_CHUNK * 4
    return pl.pallas_call(
        _mix_kernel,
        grid=(m // tm,),
        in_specs=[rows(NA_WIDTH), rows(CONV_WIDTH), rows(MEM_WIDTH),
                  rows(d, gate_col), rows(d, gate_col + 1), rows(d, gate_col + 2),
                  whole(w_pa), whole(w_pb), whole(w_pc)],
        out_specs=rows(d),
        out_shape=jax.ShapeDtypeStruct((m, d), BF16),
        compiler_params=_params(("parallel",), vmem),
        name="mix",
    )(y_na, y_conv, y_mem, h, h, h, w_pa, w_pb, w_pc)


def _proj_ln_kernel(mixed_ref, x_ref, wo_ref, g_ref, b_ref, o_ref, *, alpha):
    z = alpha * x_ref[...] + jnp.dot(mixed_ref[...], wo_ref[...], preferred_element_type=F32)
    o_ref[...] = _layer_norm(z, g_ref[...], b_ref[...])


def _proj_ln(mixed, x, w_o, ln_g, ln_b, *, alpha):
    m, d = x.shape
    tm = min(PROJ_TM, m)
    assert m % tm == 0
    vmem = 2 * tm * d * (2 + 4 + 4) + d * d * 2 + 2 * tm * d * 4
    return pl.pallas_call(
        functools.partial(_proj_ln_kernel, alpha=alpha),
        grid=(m // tm,),
        in_specs=[pl.BlockSpec((tm, d), lambda i: (i, 0)),
                  pl.BlockSpec((tm, d), lambda i: (i, 0)),
                  pl.BlockSpec((d, d), lambda i: (0, 0), pipeline_mode=pl.Buffered(1)),
                  pl.BlockSpec((1, d), lambda i: (0, 0)),
                  pl.BlockSpec((1, d), lambda i: (0, 0))],
        out_specs=pl.BlockSpec((tm, d), lambda i: (i, 0)),
        out_shape=jax.ShapeDtypeStruct((m, d), F32),
        compiler_params=_params(("parallel",), vmem),
        name="proj_ln",
    )(mixed, x, w_o, ln_g.reshape(1, d), ln_b.reshape(1, d))


def _ffn_kernel(x_ref, wg_ref, wu_ref, wout_ref, g_ref, b_ref, o_ref, xb_ref, acc_ref, *, alpha):
    j = pl.program_id(1)

    @pl.when(j == 0)
    def _init():
        xb_ref[...] = x_ref[...].astype(BF16)
        acc_ref[...] = jnp.zeros_like(acc_ref)

    xb = xb_ref[...]
    gate = jnp.dot(xb, wg_ref[...], preferred_element_type=F32)
    up = jnp.dot(xb, wu_ref[...], preferred_element_type=F32)
    act = (gate * _sigmoid(gate)) * up
    acc_ref[...] += jnp.dot(act.astype(BF16), wout_ref[...], preferred_element_type=F32)

    @pl.when(j == pl.num_programs(1) - 1)
    def _finish():
        z = alpha * x_ref[...] + acc_ref[...]
        o_ref[...] = _layer_norm(z, g_ref[...], b_ref[...])


def _ffn(x, w_in, w_out, ln_g, ln_b, *, alpha):
    m, d = x.shape
    f = w_out.shape[0]
    tm = min(FFN_TM, m)
    tf = FFN_TF
    assert m % tm == 0 and f % tf == 0
    nf = f // tf
    vmem = 2 * tm * d * 4 * 2 + tm * d * (2 + 4) + 2 * 3 * d * tf * 2 + 4 * tm * tf * 4
    return pl.pallas_call(
        functools.partial(_ffn_kernel, alpha=alpha),
        grid=(m // tm, nf),
        in_specs=[pl.BlockSpec((tm, d), lambda i, j: (i, 0)),
                  pl.BlockSpec((d, tf), lambda i, j: (0, j)),
                  pl.BlockSpec((d, tf), lambda i, j: (0, nf + j)),
                  pl.BlockSpec((tf, d), lambda i, j: (j, 0)),
                  pl.BlockSpec((1, d), lambda i, j: (0, 0)),
                  pl.BlockSpec((1, d), lambda i, j: (0, 0))],
        out_specs=pl.BlockSpec((tm, d), lambda i, j: (i, 0)),
        out_shape=jax.ShapeDtypeStruct((m, d), F32),
        scratch_shapes=[pltpu.VMEM((tm, d), BF16), pltpu.VMEM((tm, d), F32)],
        compiler_params=_params(("parallel", "arbitrary"), vmem),
        name="ffn",
    )(x, w_in, w_in, w_out, ln_g.reshape(1, d), ln_b.reshape(1, d))


def _regroup_w_in(w_in):
    d = w_in.shape[0]
    splits = np.cumsum([NA_WIDTH, NA_WIDTH, NA_WIDTH, 2 * CONV_WIDTH, MEM_WIDTH, d, d])
    q, k, v, u, qm, g_na, g_conv, g_mem = jnp.split(w_in, splits, axis=1)
    cols = jnp.concatenate([g_na, g_conv, g_mem, q, k, v, u, qm], axis=1).astype(BF16)
    pad = -cols.shape[1] % IN_PROJ_TN
    return jnp.pad(cols, ((0, 0), (0, pad)))


def _encoder_layer(x, mem, b, t, p, *, alpha):
    d = x.shape[1]
    gate_col = 0
    qkv_off = 3 * d
    q_col = qkv_off // V7X_LANES
    k_col = (qkv_off + NA_WIDTH) // V7X_LANES
    v_col = (qkv_off + 2 * NA_WIDTH) // V7X_LANES
    u_off = qkv_off + 3 * NA_WIDTH
    assert u_off % CONV_WIDTH == 0
    a_col = u_off // CONV_WIDTH
    qm_off = u_off + 2 * CONV_WIDTH
    assert qm_off % MEM_HEAD_DIM == 0 and qkv_off % V7X_LANES == 0
    qm_col = qm_off // MEM_HEAD_DIM

    h = _matmul(x, p["w_in"], tm=IN_PROJ_TM, tn=IN_PROJ_TN, name="in_proj")
    h3 = h.reshape(b, t, h.shape[1])
    y_na = _na_attention(h3, p["na_bias"], q_col=q_col, k_col=k_col, v_col=v_col)
    y_conv = _conv_module(h3, p["conv_w"], p["conv_b"], p["conv_ln_g"], p["conv_ln_b"],
                          a_col=a_col, g_col=a_col + 1)
    mt = mem.shape[1]
    kv = _matmul(mem.reshape(b * mt, d), p["w_mem_kv"], tm=b * mt, tn=IN_PROJ_TN, name="mem_kv")
    y_mem = _mem_attention(h3, kv.reshape(b, mt, 2 * MEM_WIDTH), q_col=qm_col)
    mixed = _mix(y_na.reshape(b * t, NA_WIDTH), y_conv.reshape(b * t, CONV_WIDTH),
                 y_mem.reshape(b * t, MEM_WIDTH), h, p["w_pa"], p["w_pb"], p["w_pc"], gate_col=gate_col)
    x = _proj_ln(mixed, x, p["w_o"], p["ln1_g"], p["ln1_b"], alpha=alpha)
    return _ffn(x, p["w_ffn_in"], p["w_ffn_out"], p["ln2_g"], p["ln2_b"], alpha=alpha)


def _layer_params(l, w_in, w_mem_kv, rpb, conv_w, conv_b, conv_ln_g, conv_ln_b, w_pa, w_pb, w_pc, w_o,
                  ln1_g, ln1_b, w_ffn_in, w_ffn_out, ln2_g, ln2_b):
    return dict(
        w_in=_regroup_w_in(w_in[l]),
        w_mem_kv=w_mem_kv[l].astype(BF16),
        na_bias=_na_bias_table(rpb[l]),
        conv_w=conv_w[l], conv_b=conv_b[l], conv_ln_g=conv_ln_g[l], conv_ln_b=conv_ln_b[l],
        w_pa=w_pa[l].astype(BF16), w_pb=w_pb[l].astype(BF16), w_pc=w_pc[l].astype(BF16),
        w_o=w_o[l].astype(BF16), ln1_g=ln1_g[l], ln1_b=ln1_b[l],
        w_ffn_in=w_ffn_in[l].astype(BF16), w_ffn_out=w_ffn_out[l].astype(BF16),
        ln2_g=ln2_g[l], ln2_b=ln2_b[l],
    )


def _run_trunk(x, mem, layers, alpha):
    b, t, d = x.shape
    y = x.reshape(b * t, d)
    for p in layers:
        y = _encoder_layer(y, mem, b, t, p, alpha=alpha)
    return y.reshape(b, t, d)


def kernel(x_prompt, x_sample, mem_prompt, mem_sample, w_in, w_mem_kv, rpb, conv_w, conv_b, conv_ln_g,
           conv_ln_b, w_pa, w_pb, w_pc, w_o, ln1_g, ln1_b, w_ffn_in, w_ffn_out, ln2_g, ln2_b):
    depth = w_in.shape[0]
    alpha = (2 * depth) ** 0.25
    weights = (w_in, w_mem_kv, rpb, conv_w, conv_b, conv_ln_g, conv_ln_b, w_pa, w_pb, w_pc, w_o,
               ln1_g, ln1_b, w_ffn_in, w_ffn_out, ln2_g, ln2_b)
    layers = [_layer_params(l, *weights) for l in range(depth)]
    y_prompt = _run_trunk(x_prompt, mem_prompt, layers, alpha)
    y_sample = _run_trunk(x_sample, mem_sample, layers, alpha)
    return (y_prompt, y_sample)
```

```python
import functools

import jax
import jax.numpy as jnp
import numpy as np
from jax import lax
from jax.experimental import pallas as pl
from jax.experimental.pallas import tpu as pltpu

F32 = jnp.float32
BF16 = jnp.bfloat16

GRID_W = 64
NA_HEADS = 12
NA_HEAD_DIM = 64
NA_WIDTH = NA_HEADS * NA_HEAD_DIM
NA_WIN_ROWS = 8
NA_WIN_COLS = 16
RPB_ROWS = 2 * NA_WIN_ROWS - 1
RPB_COLS = 2 * NA_WIN_COLS - 1
CONV_WIDTH = 768
CONV_KERNEL = 31
MEM_HEADS = 4
MEM_HEAD_DIM = 128
MEM_WIDTH = MEM_HEADS * MEM_HEAD_DIM
LN_EPS = 1e-5
MASK_VALUE = -1e30

V7X_LANES = 128
V7X_SUBLANES = 8
V7X_VMEM_BYTES = 64 * 1024 * 1024
V7X_VMEM_RESERVE_BYTES = 8 * 1024 * 1024
KERNEL_TEMP_BYTES = 16 * 1024 * 1024

IN_PROJ_TM = 1024
IN_PROJ_TN = 512
NA_ROWS_PER_STEP = 8
CONV_TT = 256
CONV_HALO = 16
CONV_ROW_CHUNK = 64
MEM_TQ = 1024
MIX_TM = 512
MIX_CHUNK = 512
PROJ_TM = 512
FFN_TM = 512
FFN_TF = 512


def _params(semantics, block_bytes):
    limit = min(block_bytes + KERNEL_TEMP_BYTES, V7X_VMEM_BYTES - V7X_VMEM_RESERVE_BYTES)
    return pltpu.CompilerParams(dimension_semantics=semantics, vmem_limit_bytes=int(limit))


def _sigmoid(x):
    return 1.0 / (1.0 + jnp.exp(-x))


def _layer_norm(z, g, b):
    mu = jnp.mean(z, axis=-1, keepdims=True)
    zc = z - mu
    var = jnp.mean(zc * zc, axis=-1, keepdims=True)
    return zc * lax.rsqrt(var + LN_EPS) * g + b


def _matmul_kernel(x_ref, w_ref, o_ref, xb_ref):
    @pl.when(pl.program_id(1) == 0)
    def _cast():
        xb_ref[...] = x_ref[...].astype(BF16)

    o_ref[...] = jnp.dot(xb_ref[...], w_ref[...], preferred_element_type=F32).astype(o_ref.dtype)


def _matmul(x, w, *, tm, tn, name):
    m, k = x.shape
    n = w.shape[1]
    tm = min(tm, m)
    assert m % tm == 0 and n % tn == 0
    vmem = 2 * tm * k * 4 + tm * k * 2 + 2 * k * tn * 2 + 2 * tm * tn * 2
    return pl.pallas_call(
        _matmul_kernel,
        grid=(m // tm, n // tn),
        in_specs=[pl.BlockSpec((tm, k), lambda i, j: (i, 0)),
                  pl.BlockSpec((k, tn), lambda i, j: (0, j))],
        out_specs=pl.BlockSpec((tm, tn), lambda i, j: (i, j)),
        out_shape=jax.ShapeDtypeStruct((m, n), BF16),
        scratch_shapes=[pltpu.VMEM((tm, k), BF16)],
        compiler_params=_params(("parallel", "arbitrary"), vmem),
        name=name,
    )(x, w)


def _na_bias_table(rpb):
    heads = rpb.shape[0]
    rpb = rpb.astype(F32)
    period = 2 * GRID_W
    zeros = jnp.zeros((heads, RPB_ROWS, period - RPB_COLS), F32)
    e = jnp.concatenate([rpb[..., NA_WIN_COLS - 1:], zeros, rpb[..., :NA_WIN_COLS - 1]], axis=-1)
    tz = jnp.tile(e, (1, 1, GRID_W))[..., :GRID_W * (period - 1)]
    tz = tz.reshape(heads, RPB_ROWS, GRID_W, period - 1)[..., :GRID_W]
    t = jnp.stack([tz[:, NA_WIN_ROWS - 1 - d:2 * NA_WIN_ROWS - 1 - d] for d in range(NA_WIN_ROWS)], axis=1)
    t = t.transpose(0, 1, 3, 2, 4)
    q = np.arange(GRID_W)[:, None]
    c = np.arange(GRID_W)[None, :]
    col_start = np.clip(q - NA_WIN_COLS // 2, 0, GRID_W - NA_WIN_COLS)
    valid = (c >= col_start) & (c < col_start + NA_WIN_COLS)
    t = jnp.where(valid[None, None, :, None, :], t, MASK_VALUE)
    t = t.reshape(heads // 2, 2, NA_WIN_ROWS, GRID_W, NA_WIN_ROWS * GRID_W)
    return t.transpose(0, 2, 1, 3, 4).reshape(heads // 2, NA_WIN_ROWS, 2 * GRID_W, NA_WIN_ROWS * GRID_W)


def _na_kernel(q_ref, k_ref, v_ref, bias_ref, o_ref, *, n_rows):
    rb = pl.program_id(2)
    win = NA_WIN_ROWS * GRID_W
    lane = lax.broadcasted_iota(jnp.int32, (GRID_W, 2 * NA_HEAD_DIM), 1)
    first_head = lane < NA_HEAD_DIM
    scale = NA_HEAD_DIM ** -0.5

    def window_start(i):
        r = rb * NA_ROWS_PER_STEP + i
        start = jnp.clip(r - NA_WIN_ROWS // 2, 0, n_rows - NA_WIN_ROWS)
        return pl.multiple_of(start * GRID_W, GRID_W), r - start

    def scores(i):
        tok, shift = window_start(i)
        q = q_ref[0, i * GRID_W:(i + 1) * GRID_W, :] * scale
        kw = k_ref[0, pl.ds(tok, win), :]
        zero = jnp.zeros_like(q)
        q2 = jnp.concatenate([jnp.where(first_head, q, zero), jnp.where(first_head, zero, q)], axis=0)
        s = lax.dot_general(q2, kw, (((1,), (1,)), ((), ())), preferred_element_type=F32)
        return s + bias_ref[0, shift]

    s_next = scores(0)
    for i in range(NA_ROWS_PER_STEP):
        s = s_next
        if i + 1 < NA_ROWS_PER_STEP:
            s_next = scores(i + 1)
        tok, _ = window_start(i)
        vw = v_ref[0, pl.ds(tok, win), :]
        m = jnp.max(s, axis=-1, keepdims=True)
        p = jnp.exp(s - m)
        l = jnp.sum(p, axis=-1, keepdims=True)
        o2 = jnp.dot(p.astype(BF16), vw, preferred_element_type=F32) / l
        o = jnp.where(first_head, o2[:GRID_W], o2[GRID_W:])
        o_ref[0, i * GRID_W:(i + 1) * GRID_W, :] = o.astype(o_ref.dtype)


def _na_attention(h3, bias, *, q_col, k_col, v_col):
    b, t, _ = h3.shape
    n_rows = t // GRID_W
    pairs = NA_HEADS // 2
    tq = NA_ROWS_PER_STEP * GRID_W
    assert n_rows % NA_ROWS_PER_STEP == 0 and n_rows >= NA_WIN_ROWS
    lanes = 2 * NA_HEAD_DIM
    bias_bytes = NA_WIN_ROWS * 2 * GRID_W * NA_WIN_ROWS * GRID_W * 4
    vmem = 2 * (2 * tq * lanes * 2 + 2 * t * lanes * 2 + bias_bytes)
    return pl.pallas_call(
        functools.partial(_na_kernel, n_rows=n_rows),
        grid=(b, pairs, n_rows // NA_ROWS_PER_STEP),
        in_specs=[
            pl.BlockSpec((1, tq, lanes), lambda bi, hp, rb: (bi, rb, q_col + hp)),
            pl.BlockSpec((1, t, lanes), lambda bi, hp, rb: (bi, 0, k_col + hp)),
            pl.BlockSpec((1, t, lanes), lambda bi, hp, rb: (bi, 0, v_col + hp)),
            pl.BlockSpec((1, NA_WIN_ROWS, 2 * GRID_W, NA_WIN_ROWS * GRID_W), lambda bi, hp, rb: (hp, 0, 0, 0)),
        ],
        out_specs=pl.BlockSpec((1, tq, lanes), lambda bi, hp, rb: (bi, rb, hp)),
        out_shape=jax.ShapeDtypeStruct((b, t, NA_WIDTH), BF16),
        compiler_params=_params(("parallel", "parallel", "arbitrary"), vmem),
        name="na_attn",
    )(h3, h3, h3, bias)


def _conv_kernel(a_ref, g_ref, ap_ref, gp_ref, an_ref, gn_ref, w_ref, cb_ref, lg_ref, lb_ref, o_ref, u_ref,
                 c_ref, *, n_tiles):
    ti = pl.program_id(1)
    tt = a_ref.shape[1]
    pad = CONV_KERNEL // 2

    def glu(a, g):
        return a.astype(F32) * _sigmoid(g.astype(F32))

    u_ref[CONV_HALO:CONV_HALO + tt, :] = glu(a_ref[0], g_ref[0])
    u_ref[0:CONV_HALO, :] = jnp.where(ti > 0, glu(ap_ref[0], gp_ref[0]), 0.0)
    u_ref[CONV_HALO + tt:2 * CONV_HALO + tt, :] = jnp.where(ti < n_tiles - 1, glu(an_ref[0], gn_ref[0]), 0.0)

    rc = CONV_ROW_CHUNK

    def chunk(ci, carry):
        base = pl.multiple_of(ci * rc, rc)
        for lc in range(CONV_WIDTH // V7X_LANES):
            ls = slice(lc * V7X_LANES, (lc + 1) * V7X_LANES)
            acc = None
            for r in range(V7X_SUBLANES):
                part = None
                for a in range(-2, 2):
                    j = V7X_SUBLANES * a + r + pad
                    if not 0 <= j < CONV_KERNEL:
                        continue
                    rows = pl.ds(pl.multiple_of(base + CONV_HALO + V7X_SUBLANES * a, V7X_SUBLANES),
                                 rc + V7X_SUBLANES)
                    term = u_ref[rows, ls] * w_ref[j:j + 1, ls]
                    part = term if part is None else part + term
                part = part[r:r + rc]
                acc = part if acc is None else acc + part
            c_ref[:, ls] = acc
        y = _layer_norm(c_ref[...] + cb_ref[...], lg_ref[...], lb_ref[...])
        y = y * _sigmoid(y)
        o_ref[0, pl.ds(base, rc), :] = y.astype(o_ref.dtype)
        return carry

    lax.fori_loop(0, tt // rc, chunk, 0)


def _conv_module(h3, conv_w, conv_b, ln_g, ln_b, *, a_col, g_col):
    b, t, _ = h3.shape
    tt = min(CONV_TT, t)
    assert t % tt == 0 and tt % CONV_HALO == 0 and tt % CONV_ROW_CHUNK == 0
    n_tiles = t // tt
    hpt = tt // CONV_HALO
    last_halo = t // CONV_HALO - 1
    cw = CONV_WIDTH

    def cur(col):
        return pl.BlockSpec((1, tt, cw), lambda bi, ti: (bi, ti, col))

    def prev(col):
        return pl.BlockSpec((1, CONV_HALO, cw), lambda bi, ti: (bi, jnp.maximum(ti * hpt - 1, 0), col))

    def nxt(col):
        return pl.BlockSpec((1, CONV_HALO, cw), lambda bi, ti: (bi, jnp.minimum((ti + 1) * hpt, last_halo), col))

    def vec(rows):
        return pl.BlockSpec((rows, cw), lambda bi, ti: (0, 0))

    vmem = 2 * (2 * tt * cw * 2 + 4 * CONV_HALO * cw * 2 + tt * cw * 2) + (tt + 2 * CONV_HALO) * cw * 4 \
        + 2 * (CONV_KERNEL + 3) * cw * 4
    return pl.pallas_call(
        functools.partial(_conv_kernel, n_tiles=n_tiles),
        grid=(b, n_tiles),
        in_specs=[cur(a_col), cur(g_col), prev(a_col), prev(g_col), nxt(a_col), nxt(g_col),
                  vec(CONV_KERNEL), vec(1), vec(1), vec(1)],
        out_specs=pl.BlockSpec((1, tt, cw), lambda bi, ti: (bi, ti, 0)),
        out_shape=jax.ShapeDtypeStruct((b, t, cw), BF16),
        scratch_shapes=[pltpu.VMEM((tt + 2 * CONV_HALO, cw), F32), pltpu.VMEM((CONV_ROW_CHUNK, cw), F32)],
        compiler_params=_params(("parallel", "parallel"), vmem),
        name="conv_mod",
    )(h3, h3, h3, h3, h3, h3, conv_w, conv_b.reshape(1, cw), ln_g.reshape(1, cw), ln_b.reshape(1, cw))


def _mem_attn_kernel(q_ref, k_ref, v_ref, o_ref):
    s = lax.dot_general(q_ref[0], k_ref[0], (((1,), (1,)), ((), ())), preferred_element_type=F32)
    s = s * (MEM_HEAD_DIM ** -0.5)
    m = jnp.max(s, axis=-1, keepdims=True)
    p = jnp.exp(s - m)
    l = jnp.sum(p, axis=-1, keepdims=True)
    o = jnp.dot(p.astype(BF16), v_ref[0], preferred_element_type=F32) / l
    o_ref[0] = o.astype(o_ref.dtype)


def _mem_attention(h3, kv3, *, q_col):
    b, t, _ = h3.shape
    mt = kv3.shape[1]
    tq = min(MEM_TQ, t)
    assert t % tq == 0
    hd = MEM_HEAD_DIM
    vmem = 2 * (2 * tq * hd * 2 + 2 * mt * hd * 2) + 3 * tq * mt * 4
    return pl.pallas_call(
        _mem_attn_kernel,
        grid=(b, t // tq, MEM_HEADS),
        in_specs=[pl.BlockSpec((1, tq, hd), lambda bi, ti, hi: (bi, ti, q_col + hi)),
                  pl.BlockSpec((1, mt, hd), lambda bi, ti, hi: (bi, 0, hi)),
                  pl.BlockSpec((1, mt, hd), lambda bi, ti, hi: (bi, 0, MEM_HEADS + hi))],
        out_specs=pl.BlockSpec((1, tq, hd), lambda bi, ti, hi: (bi, ti, hi)),
        out_shape=jax.ShapeDtypeStruct((b, t, MEM_WIDTH), BF16),
        compiler_params=_params(("parallel", "parallel", "parallel"), vmem),
        name="mem_attn",
    )(h3, kv3, kv3)


def _mix_kernel(yna_ref, yconv_ref, ymem_ref, gna_ref, gconv_ref, gmem_ref, wpa_ref, wpb_ref, wpc_ref, o_ref):
    d = o_ref.shape[1]
    for n in range(d // MIX_CHUNK):
        sl = slice(n * MIX_CHUNK, (n + 1) * MIX_CHUNK)
        ya = jnp.dot(yna_ref[...], wpa_ref[:, sl], preferred_element_type=F32)
        yb = jnp.dot(yconv_ref[...], wpb_ref[:, sl], preferred_element_type=F32)
        yc = jnp.dot(ymem_ref[...], wpc_ref[:, sl], preferred_element_type=F32)
        mixed = (_sigmoid(gna_ref[:, sl].astype(F32)) * ya
                 + _sigmoid(gconv_ref[:, sl].astype(F32)) * yb
                 + _sigmoid(gmem_ref[:, sl].astype(F32)) * yc)
        o_ref[:, sl] = mixed.astype(o_ref.dtype)


def _mix(y_na, y_conv, y_mem, h, w_pa, w_pb, w_pc, *, gate_col):
    m = y_na.shape[0]
    d = w_pa.shape[1]
    tm = min(MIX_TM, m)
    assert m % tm == 0 and d % MIX_CHUNK == 0

    def rows(width, col=0):
        return pl.BlockSpec((tm, width), lambda i: (i, col))

    def whole(w):
        return pl.BlockSpec(w.shape, lambda i: (0, 0), pipeline_mode=pl.Buffered(1))

    vmem = 2 * tm * (2 * NA_WIDTH + MEM_WIDTH + 3 * d + d) * 2 + (2 * NA_WIDTH + MEM_WIDTH) * d * 2 \
        + 6 * tm * MIX_CHUNK * 4
    return pl.pallas_call(
        _mix_kernel,
        grid=(m // tm,),
        in_specs=[rows(NA_WIDTH), rows(CONV_WIDTH), rows(MEM_WIDTH),
                  rows(d, gate_col), rows(d, gate_col + 1), rows(d, gate_col + 2),
                  whole(w_pa), whole(w_pb), whole(w_pc)],
        out_specs=rows(d),
        out_shape=jax.ShapeDtypeStruct((m, d), BF16),
        compiler_params=_params(("parallel",), vmem),
        name="mix",
    )(y_na, y_conv, y_mem, h, h, h, w_pa, w_pb, w_pc)


def _proj_ln_kernel(mixed_ref, x_ref, wo_ref, g_ref, b_ref, o_ref, *, alpha):
    z = alpha * x_ref[...] + jnp.dot(mixed_ref[...], wo_ref[...], preferred_element_type=F32)
    o_ref[...] = _layer_norm(z, g_ref[...], b_ref[...])


def _proj_ln(mixed, x, w_o, ln_g, ln_b, *, alpha):
    m, d = x.shape
    tm = min(PROJ_TM, m)
    assert m % tm == 0
    vmem = 2 * tm * d * (2 + 4 + 4) + d * d * 2 + 2 * tm * d * 4
    return pl.pallas_call(
        functools.partial(_proj_ln_kernel, alpha=alpha),
        grid=(m // tm,),
        in_specs=[pl.BlockSpec((tm, d), lambda i: (i, 0)),
                  pl.BlockSpec((tm, d), lambda i: (i, 0)),
                  pl.BlockSpec((d, d), lambda i: (0, 0), pipeline_mode=pl.Buffered(1)),
                  pl.BlockSpec((1, d), lambda i: (0, 0)),
                  pl.BlockSpec((1, d), lambda i: (0, 0))],
        out_specs=pl.BlockSpec((tm, d), lambda i: (i, 0)),
        out_shape=jax.ShapeDtypeStruct((m, d), F32),
        compiler_params=_params(("parallel",), vmem),
        name="proj_ln",
    )(mixed, x, w_o, ln_g.reshape(1, d), ln_b.reshape(1, d))


def _ffn_kernel(x_ref, wg_ref, wu_ref, wout_ref, g_ref, b_ref, o_ref, xb_ref, acc_ref, *, alpha):
    j = pl.program_id(1)

    @pl.when(j == 0)
    def _init():
        xb_ref[...] = x_ref[...].astype(BF16)
        acc_ref[...] = jnp.zeros_like(acc_ref)

    xb = xb_ref[...]
    gate = jnp.dot(xb, wg_ref[...], preferred_element_type=F32)
    up = jnp.dot(xb, wu_ref[...], preferred_element_type=F32)
    act = (gate * _sigmoid(gate)) * up
    acc_ref[...] += jnp.dot(act.astype(BF16), wout_ref[...], preferred_element_type=F32)

    @pl.when(j == pl.num_programs(1) - 1)
    def _finish():
        z = alpha * x_ref[...] + acc_ref[...]
        o_ref[...] = _layer_norm(z, g_ref[...], b_ref[...])


def _ffn(x, w_in, w_out, ln_g, ln_b, *, alpha):
    m, d = x.shape
    f = w_out.shape[0]
    tm = min(FFN_TM, m)
    tf = FFN_TF
    assert m % tm == 0 and f % tf == 0
    nf = f // tf
    vmem = 2 * tm * d * 4 * 2 + tm * d * (2 + 4) + 2 * 3 * d * tf * 2 + 4 * tm * tf * 4
    return pl.pallas_call(
        functools.partial(_ffn_kernel, alpha=alpha),
        grid=(m // tm, nf),
        in_specs=[pl.BlockSpec((tm, d), lambda i, j: (i, 0)),
                  pl.BlockSpec((d, tf), lambda i, j: (0, j)),
                  pl.BlockSpec((d, tf), lambda i, j: (0, nf + j)),
                  pl.BlockSpec((tf, d), lambda i, j: (j, 0)),
                  pl.BlockSpec((1, d), lambda i, j: (0, 0)),
                  pl.BlockSpec((1, d), lambda i, j: (0, 0))],
        out_specs=pl.BlockSpec((tm, d), lambda i, j: (i, 0)),
        out_shape=jax.ShapeDtypeStruct((m, d), F32),
        scratch_shapes=[pltpu.VMEM((tm, d), BF16), pltpu.VMEM((tm, d), F32)],
        compiler_params=_params(("parallel", "arbitrary"), vmem),
        name="ffn",
    )(x, w_in, w_in, w_out, ln_g.reshape(1, d), ln_b.reshape(1, d))


def _regroup_w_in(w_in):
    d = w_in.shape[0]
    splits = np.cumsum([NA_WIDTH, NA_WIDTH, NA_WIDTH, 2 * CONV_WIDTH, MEM_WIDTH, d, d])
    q, k, v, u, qm, g_na, g_conv, g_mem = jnp.split(w_in, splits, axis=1)
    cols = jnp.concatenate([g_na, g_conv, g_mem, q, k, v, u, qm], axis=1).astype(BF16)
    pad = -cols.shape[1] % IN_PROJ_TN
    return jnp.pad(cols, ((0, 0), (0, pad)))


def _encoder_layer(x, mem, b, t, p, *, alpha):
    d = x.shape[1]
    gate_col = 0
    qkv_off = 3 * d
    q_col = qkv_off // V7X_LANES
    k_col = (qkv_off + NA_WIDTH) // V7X_LANES
    v_col = (qkv_off + 2 * NA_WIDTH) // V7X_LANES
    u_off = qkv_off + 3 * NA_WIDTH
    assert u_off % CONV_WIDTH == 0
    a_col = u_off // CONV_WIDTH
    qm_off = u_off + 2 * CONV_WIDTH
    assert qm_off % MEM_HEAD_DIM == 0 and qkv_off % V7X_LANES == 0
    qm_col = qm_off // MEM_HEAD_DIM

    h = _matmul(x, p["w_in"], tm=IN_PROJ_TM, tn=IN_PROJ_TN, name="in_proj")
    h3 = h.reshape(b, t, h.shape[1])
    y_na = _na_attention(h3, p["na_bias"], q_col=q_col, k_col=k_col, v_col=v_col)
    y_conv = _conv_module(h3, p["conv_w"], p["conv_b"], p["conv_ln_g"], p["conv_ln_b"],
                          a_col=a_col, g_col=a_col + 1)
    mt = mem.shape[1]
    kv = _matmul(mem.reshape(b * mt, d), p["w_mem_kv"], tm=b * mt, tn=IN_PROJ_TN, name="mem_kv")
    y_mem = _mem_attention(h3, kv.reshape(b, mt, 2 * MEM_WIDTH), q_col=qm_col)
    mixed = _mix(y_na.reshape(b * t, NA_WIDTH), y_conv.reshape(b * t, CONV_WIDTH),
                 y_mem.reshape(b * t, MEM_WIDTH), h, p["w_pa"], p["w_pb"], p["w_pc"], gate_col=gate_col)
    x = _proj_ln(mixed, x, p["w_o"], p["ln1_g"], p["ln1_b"], alpha=alpha)
    return _ffn(x, p["w_ffn_in"], p["w_ffn_out"], p["ln2_g"], p["ln2_b"], alpha=alpha)


def _layer_params(l, w_in, w_mem_kv, rpb, conv_w, conv_b, conv_ln_g, conv_ln_b, w_pa, w_pb, w_pc, w_o,
                  ln1_g, ln1_b, w_ffn_in, w_ffn_out, ln2_g, ln2_b):
    return dict(
        w_in=_regroup_w_in(w_in[l]),
        w_mem_kv=w_mem_kv[l].astype(BF16),
        na_bias=_na_bias_table(rpb[l]),
        conv_w=conv_w[l], conv_b=conv_b[l], conv_ln_g=conv_ln_g[l], conv_ln_b=conv_ln_b[l],
        w_pa=w_pa[l].astype(BF16), w_pb=w_pb[l].astype(BF16), w_pc=w_pc[l].astype(BF16),
        w_o=w_o[l].astype(BF16), ln1_g=ln1_g[l], ln1_b=ln1_b[l],
        w_ffn_in=w_ffn_in[l].astype(BF16), w_ffn_out=w_ffn_out[l].astype(BF16),
        ln2_g=ln2_g[l], ln2_b=ln2_b[l],
    )


def _run_trunk(x, mem, layers, alpha):
    b, t, d = x.shape
    y = x.reshape(b * t, d)
    for p in layers:
        y = _encoder_layer(y, mem, b, t, p, alpha=alpha)
    return y.reshape(b, t, d)


def kernel(x_prompt, x_sample, mem_prompt, mem_sample, w_in, w_mem_kv, rpb, conv_w, conv_b, conv_ln_g,
           conv_ln_b, w_pa, w_pb, w_pc, w_o, ln1_g, ln1_b, w_ffn_in, w_ffn_out, ln2_g, ln2_b):
    depth = w_in.shape[0]
    alpha = (2 * depth) ** 0.25
    weights = (w_in, w_mem_kv, rpb, conv_w, conv_b, conv_ln_g, conv_ln_b, w_pa, w_pb, w_pc, w_o,
               ln1_g, ln1_b, w_ffn_in, w_ffn_out, ln2_g, ln2_b)
    layers = [_layer_params(l, *weights) for l in range(depth)]
    y_prompt = _run_trunk(x_prompt, mem_prompt, layers, alpha)
    y_sample = _run_trunk(x_sample, mem_sample, layers, alpha)
    return (y_prompt, y_sample)
```

```python
import functools

import jax
import jax.numpy as jnp
import numpy as np
from jax import lax
from jax.experimental import pallas as pl
from jax.experimental.pallas import tpu as pltpu

F32 = jnp.float32
BF16 = jnp.bfloat16

GRID_W = 64
NA_HEADS = 12
NA_HEAD_DIM = 64
NA_WIDTH = NA_HEADS * NA_HEAD_DIM
NA_WIN_ROWS = 8
NA_WIN_COLS = 16
RPB_ROWS = 2 * NA_WIN_ROWS - 1
RPB_COLS = 2 * NA_WIN_COLS - 1
CONV_WIDTH = 768
CONV_KERNEL = 31
MEM_HEADS = 4
MEM_HEAD_DIM = 128
MEM_WIDTH = MEM_HEADS * MEM_HEAD_DIM
LN_EPS = 1e-5
MASK_VALUE = -1e30

V7X_LANES = 128
V7X_SUBLANES = 8
V7X_VMEM_BYTES = 64 * 1024 * 1024
V7X_VMEM_RESERVE_BYTES = 8 * 1024 * 1024
KERNEL_TEMP_BYTES = 16 * 1024 * 1024

IN_PROJ_TM = 1024
IN_PROJ_TN = 1536
MEM_KV_TN = 512
NA_ROWS_PER_STEP = 16
CONV_TT = 256
CONV_HALO = 16
CONV_ROW_CHUNK = 64
MEM_TQ = 1024
MIX_TM = 512
MIX_CHUNK = 512
PROJ_TM = 512
FFN_TM = 1024
FFN_TF = 512


def _params(semantics, block_bytes):
    limit = min(block_bytes + KERNEL_TEMP_BYTES, V7X_VMEM_BYTES - V7X_VMEM_RESERVE_BYTES)
    return pltpu.CompilerParams(dimension_semantics=semantics, vmem_limit_bytes=int(limit))


def _sigmoid(x):
    return 1.0 / (1.0 + jnp.exp(-x))


def _layer_norm(z, g, b):
    mu = jnp.mean(z, axis=-1, keepdims=True)
    zc = z - mu
    var = jnp.mean(zc * zc, axis=-1, keepdims=True)
    return zc * lax.rsqrt(var + LN_EPS) * g + b


def _matmul_kernel(x_ref, w_ref, o_ref, xb_ref):
    @pl.when(pl.program_id(1) == 0)
    def _cast():
        xb_ref[...] = x_ref[...].astype(BF16)

    o_ref[...] = jnp.dot(xb_ref[...], w_ref[...], preferred_element_type=F32).astype(o_ref.dtype)


def _matmul(x, w, *, tm, tn, name):
    m, k = x.shape
    n = w.shape[1]
    tm = min(tm, m)
    assert m % tm == 0 and n % tn == 0
    vmem = 2 * tm * k * 4 + tm * k * 2 + 2 * k * tn * 2 + 2 * tm * tn * 2
    return pl.pallas_call(
        _matmul_kernel,
        grid=(m // tm, n // tn),
        in_specs=[pl.BlockSpec((tm, k), lambda i, j: (i, 0)),
                  pl.BlockSpec((k, tn), lambda i, j: (0, j))],
        out_specs=pl.BlockSpec((tm, tn), lambda i, j: (i, j)),
        out_shape=jax.ShapeDtypeStruct((m, n), BF16),
        scratch_shapes=[pltpu.VMEM((tm, k), BF16)],
        compiler_params=_params(("parallel", "arbitrary"), vmem),
        name=name,
    )(x, w)


def _na_bias_table(rpb):
    heads = rpb.shape[0]
    rpb = rpb.astype(F32)
    period = 2 * GRID_W
    zeros = jnp.zeros((heads, RPB_ROWS, period - RPB_COLS), F32)
    e = jnp.concatenate([rpb[..., NA_WIN_COLS - 1:], zeros, rpb[..., :NA_WIN_COLS - 1]], axis=-1)
    tz = jnp.tile(e, (1, 1, GRID_W))[..., :GRID_W * (period - 1)]
    tz = tz.reshape(heads, RPB_ROWS, GRID_W, period - 1)[..., :GRID_W]
    t = jnp.stack([tz[:, NA_WIN_ROWS - 1 - d:2 * NA_WIN_ROWS - 1 - d] for d in range(NA_WIN_ROWS)], axis=1)
    t = t.transpose(0, 1, 3, 2, 4)
    q = np.arange(GRID_W)[:, None]
    c = np.arange(GRID_W)[None, :]
    col_start = np.clip(q - NA_WIN_COLS // 2, 0, GRID_W - NA_WIN_COLS)
    valid = (c >= col_start) & (c < col_start + NA_WIN_COLS)
    t = jnp.where(valid[None, None, :, None, :], t, MASK_VALUE)
    t = t.reshape(heads // 2, 2, NA_WIN_ROWS, GRID_W, NA_WIN_ROWS * GRID_W)
    return t.transpose(0, 2, 1, 3, 4).reshape(heads // 2, NA_WIN_ROWS, 2 * GRID_W, NA_WIN_ROWS * GRID_W)


def _na_kernel(q_ref, k_ref, v_ref, bias_ref, o_ref, *, n_rows):
    rb = pl.program_id(2)
    win = NA_WIN_ROWS * GRID_W
    lane = lax.broadcasted_iota(jnp.int32, (GRID_W, 2 * NA_HEAD_DIM), 1)
    first_head = lane < NA_HEAD_DIM
    scale = NA_HEAD_DIM ** -0.5

    def window_start(i):
        r = rb * NA_ROWS_PER_STEP + i
        start = jnp.clip(r - NA_WIN_ROWS // 2, 0, n_rows - NA_WIN_ROWS)
        return pl.multiple_of(start * GRID_W, GRID_W), r - start

    def scores(i):
        tok, shift = window_start(i)
        q = q_ref[0, i * GRID_W:(i + 1) * GRID_W, :] * scale
        kw = k_ref[0, pl.ds(tok, win), :]
        zero = jnp.zeros_like(q)
        q2 = jnp.concatenate([jnp.where(first_head, q, zero), jnp.where(first_head, zero, q)], axis=0)
        s = lax.dot_general(q2, kw, (((1,), (1,)), ((), ())), preferred_element_type=F32)
        return s + bias_ref[0, shift]

    s_next = scores(0)
    for i in range(NA_ROWS_PER_STEP):
        s = s_next
        if i + 1 < NA_ROWS_PER_STEP:
            s_next = scores(i + 1)
        tok, _ = window_start(i)
        vw = v_ref[0, pl.ds(tok, win), :]
        m = jnp.max(s, axis=-1, keepdims=True)
        p = jnp.exp(s - m)
        l = jnp.sum(p, axis=-1, keepdims=True)
        o2 = jnp.dot(p.astype(BF16), vw, preferred_element_type=F32) / l
        o = jnp.where(first_head, o2[:GRID_W], o2[GRID_W:])
        o_ref[0, i * GRID_W:(i + 1) * GRID_W, :] = o.astype(o_ref.dtype)


def _na_attention(h3, bias, *, q_col, k_col, v_col):
    b, t, _ = h3.shape
    n_rows = t // GRID_W
    pairs = NA_HEADS // 2
    tq = NA_ROWS_PER_STEP * GRID_W
    assert n_rows % NA_ROWS_PER_STEP == 0 and n_rows >= NA_WIN_ROWS
    lanes = 2 * NA_HEAD_DIM
    bias_bytes = NA_WIN_ROWS * 2 * GRID_W * NA_WIN_ROWS * GRID_W * 4
    vmem = 2 * (2 * tq * lanes * 2 + 2 * t * lanes * 2 + bias_bytes)
    return pl.pallas_call(
        functools.partial(_na_kernel, n_rows=n_rows),
        grid=(pairs, b, n_rows // NA_ROWS_PER_STEP),
        in_specs=[
            pl.BlockSpec((1, tq, lanes), lambda hp, bi, rb: (bi, rb, q_col + hp)),
            pl.BlockSpec((1, t, lanes), lambda hp, bi, rb: (bi, 0, k_col + hp)),
            pl.BlockSpec((1, t, lanes), lambda hp, bi, rb: (bi, 0, v_col + hp)),
            pl.BlockSpec((1, NA_WIN_ROWS, 2 * GRID_W, NA_WIN_ROWS * GRID_W), lambda hp, bi, rb: (hp, 0, 0, 0)),
        ],
        out_specs=pl.BlockSpec((1, tq, lanes), lambda hp, bi, rb: (bi, rb, hp)),
        out_shape=jax.ShapeDtypeStruct((b, t, NA_WIDTH), BF16),
        compiler_params=_params(("parallel", "parallel", "arbitrary"), vmem),
        name="na_attn",
    )(h3, h3, h3, bias)


def _conv_kernel(a_ref, g_ref, ap_ref, gp_ref, an_ref, gn_ref, w_ref, cb_ref, lg_ref, lb_ref, o_ref, u_ref,
                 c_ref, *, n_tiles):
    ti = pl.program_id(1)
    tt = a_ref.shape[1]
    pad = CONV_KERNEL // 2

    def glu(a, g):
        return a.astype(F32) * _sigmoid(g.astype(F32))

    u_ref[CONV_HALO:CONV_HALO + tt, :] = glu(a_ref[0], g_ref[0])
    u_ref[0:CONV_HALO, :] = jnp.where(ti > 0, glu(ap_ref[0], gp_ref[0]), 0.0)
    u_ref[CONV_HALO + tt:2 * CONV_HALO + tt, :] = jnp.where(ti < n_tiles - 1, glu(an_ref[0], gn_ref[0]), 0.0)

    rc = CONV_ROW_CHUNK

    def chunk(ci, carry):
        base = pl.multiple_of(ci * rc, rc)
        for lc in range(CONV_WIDTH // V7X_LANES):
            ls = slice(lc * V7X_LANES, (lc + 1) * V7X_LANES)
            acc = None
            for r in range(V7X_SUBLANES):
                part = None
                for a in range(-2, 2):
                    j = V7X_SUBLANES * a + r + pad
                    if not 0 <= j < CONV_KERNEL:
                        continue
                    rows = pl.ds(pl.multiple_of(base + CONV_HALO + V7X_SUBLANES * a, V7X_SUBLANES),
                                 rc + V7X_SUBLANES)
                    term = u_ref[rows, ls] * w_ref[j:j + 1, ls]
                    part = term if part is None else part + term
                part = part[r:r + rc]
                acc = part if acc is None else acc + part
            c_ref[:, ls] = acc
        y = _layer_norm(c_ref[...] + cb_ref[...], lg_ref[...], lb_ref[...])
        y = y * _sigmoid(y)
        o_ref[0, pl.ds(base, rc), :] = y.astype(o_ref.dtype)
        return carry

    lax.fori_loop(0, tt // rc, chunk, 0)


def _conv_module(h3, conv_w, conv_b, ln_g, ln_b, *, a_col, g_col):
    b, t, _ = h3.shape
    tt = min(CONV_TT, t)
    assert t % tt == 0 and tt % CONV_HALO == 0 and tt % CONV_ROW_CHUNK == 0
    n_tiles = t // tt
    hpt = tt // CONV_HALO
    last_halo = t // CONV_HALO - 1
    cw = CONV_WIDTH

    def cur(col):
        return pl.BlockSpec((1, tt, cw), lambda bi, ti: (bi, ti, col))

    def prev(col):
        return pl.BlockSpec((1, CONV_HALO, cw), lambda bi, ti: (bi, jnp.maximum(ti * hpt - 1, 0), col))

    def nxt(col):
        return pl.BlockSpec((1, CONV_HALO, cw), lambda bi, ti: (bi, jnp.minimum((ti + 1) * hpt, last_halo), col))

    def vec(rows):
        return pl.BlockSpec((rows, cw), lambda bi, ti: (0, 0))

    vmem = 2 * (2 * tt * cw * 2 + 4 * CONV_HALO * cw * 2 + tt * cw * 2) + (tt + 2 * CONV_HALO) * cw * 4 \
        + 2 * (CONV_KERNEL + 3) * cw * 4
    return pl.pallas_call(
        functools.partial(_conv_kernel, n_tiles=n_tiles),
        grid=(b, n_tiles),
        in_specs=[cur(a_col), cur(g_col), prev(a_col), prev(g_col), nxt(a_col), nxt(g_col),
                  vec(CONV_KERNEL), vec(1), vec(1), vec(1)],
        out_specs=pl.BlockSpec((1, tt, cw), lambda bi, ti: (bi, ti, 0)),
        out_shape=jax.ShapeDtypeStruct((b, t, cw), BF16),
        scratch_shapes=[pltpu.VMEM((tt + 2 * CONV_HALO, cw), F32), pltpu.VMEM((CONV_ROW_CHUNK, cw), F32)],
        compiler_params=_params(("parallel", "parallel"), vmem),
        name="conv_mod",
    )(h3, h3, h3, h3, h3, h3, conv_w, conv_b.reshape(1, cw), ln_g.reshape(1, cw), ln_b.reshape(1, cw))


def _mem_attn_kernel(q_ref, k_ref, v_ref, o_ref):
    s = lax.dot_general(q_ref[0], k_ref[0], (((1,), (1,)), ((), ())), preferred_element_type=F32)
    s = s * (MEM_HEAD_DIM ** -0.5)
    m = jnp.max(s, axis=-1, keepdims=True)
    p = jnp.exp(s - m)
    l = jnp.sum(p, axis=-1, keepdims=True)
    o = jnp.dot(p.astype(BF16), v_ref[0], preferred_element_type=F32) / l
    o_ref[0] = o.astype(o_ref.dtype)


def _mem_attention(h3, kv3, *, q_col):
    b, t, _ = h3.shape
    mt = kv3.shape[1]
    tq = min(MEM_TQ, t)
    assert t % tq == 0
    hd = MEM_HEAD_DIM
    vmem = 2 * (2 * tq * hd * 2 + 2 * mt * hd * 2) + 3 * tq * mt * 4
    return pl.pallas_call(
        _mem_attn_kernel,
        grid=(b, t // tq, MEM_HEADS),
        in_specs=[pl.BlockSpec((1, tq, hd), lambda bi, ti, hi: (bi, ti, q_col + hi)),
                  pl.BlockSpec((1, mt, hd), lambda bi, ti, hi: (bi, 0, hi)),
                  pl.BlockSpec((1, mt, hd), lambda bi, ti, hi: (bi, 0, MEM_HEADS + hi))],
        out_specs=pl.BlockSpec((1, tq, hd), lambda bi, ti, hi: (bi, ti, hi)),
        out_shape=jax.ShapeDtypeStruct((b, t, MEM_WIDTH), BF16),
        compiler_params=_params(("parallel", "parallel", "parallel"), vmem),
        name="mem_attn",
    )(h3, kv3, kv3)


def _mix_kernel(yna_ref, yconv_ref, ymem_ref, gna_ref, gconv_ref, gmem_ref, wpa_ref, wpb_ref, wpc_ref, o_ref):
    d = o_ref.shape[1]
    for n in range(d // MIX_CHUNK):
        sl = slice(n * MIX_CHUNK, (n + 1) * MIX_CHUNK)
        ya = jnp.dot(yna_ref[...], wpa_ref[:, sl], preferred_element_type=F32)
        yb = jnp.dot(yconv_ref[...], wpb_ref[:, sl], preferred_element_type=F32)
        yc = jnp.dot(ymem_ref[...], wpc_ref[:, sl], preferred_element_type=F32)
        mixed = (_sigmoid(gna_ref[:, sl].astype(F32)) * ya
                 + _sigmoid(gconv_ref[:, sl].astype(F32)) * yb
                 + _sigmoid(gmem_ref[:, sl].astype(F32)) * yc)
        o_ref[:, sl] = mixed.astype(o_ref.dtype)


def _mix(y_na, y_conv, y_mem, h, w_pa, w_pb, w_pc, *, gate_col):
    m = y_na.shape[0]
    d = w_pa.shape[1]
    tm = min(MIX_TM, m)
    assert m % tm == 0 and d % MIX_CHUNK == 0

    def rows(width, col=0):
        return pl.BlockSpec((tm, width), lambda i: (i, col))

    def whole(w):
        return pl.BlockSpec(w.shape, lambda i: (0, 0), pipeline_mode=pl.Buffered(1))

    vmem = 2 * tm * (2 * NA_WIDTH + MEM_WIDTH + 3 * d + d) * 2 + (2 * NA_WIDTH + MEM_WIDTH) * d * 2 \
        + 6 * tm * MIX_CHUNK * 4
    return pl.pallas_call(
        _mix_kernel,
        grid=(m // tm,),
        in_specs=[rows(NA_WIDTH), rows(CONV_WIDTH), rows(MEM_WIDTH),
                  rows(d, gate_col), rows(d, gate_col + 1), rows(d, gate_col + 2),
                  whole(w_pa), whole(w_pb), whole(w_pc)],
        out_specs=rows(d),
        out_shape=jax.ShapeDtypeStruct((m, d), BF16),
        compiler_params=_params(("parallel",), vmem),
        name="mix",
    )(y_na, y_conv, y_mem, h, h, h, w_pa, w_pb, w_pc)


def _proj_ln_kernel(mixed_ref, x_ref, wo_ref, g_ref, b_ref, o_ref, *, alpha):
    z = alpha * x_ref[...] + jnp.dot(mixed_ref[...], wo_ref[...], preferred_element_type=F32)
    o_ref[...] = _layer_norm(z, g_ref[...], b_ref[...])


def _proj_ln(mixed, x, w_o, ln_g, ln_b, *, alpha):
    m, d = x.shape
    tm = min(PROJ_TM, m)
    assert m % tm == 0
    vmem = 2 * tm * d * (2 + 4 + 4) + d * d * 2 + 2 * tm * d * 4
    return pl.pallas_call(
        functools.partial(_proj_ln_kernel, alpha=alpha),
        grid=(m // tm,),
        in_specs=[pl.BlockSpec((tm, d), lambda i: (i, 0)),
                  pl.BlockSpec((tm, d), lambda i: (i, 0)),
                  pl.BlockSpec((d, d), lambda i: (0, 0), pipeline_mode=pl.Buffered(1)),
                  pl.BlockSpec((1, d), lambda i: (0, 0)),
                  pl.BlockSpec((1, d), lambda i: (0, 0))],
        out_specs=pl.BlockSpec((tm, d), lambda i: (i, 0)),
        out_shape=jax.ShapeDtypeStruct((m, d), F32),
        compiler_params=_params(("parallel",), vmem),
        name="proj_ln",
    )(mixed, x, w_o, ln_g.reshape(1, d), ln_b.reshape(1, d))


def _ffn_kernel(x_hbm, wg_ref, wu_ref, wout_ref, g_ref, b_ref, o_ref, x_buf, xb_ref, x_sem, *, alpha):
    i = pl.program_id(0)
    j = pl.program_id(1)
    tm = x_buf.shape[0]

    def x_copy(tile):
        return pltpu.make_async_copy(x_hbm.at[pl.ds(pl.multiple_of(tile * tm, tm), tm), :], x_buf, x_sem)

    @pl.when((i == 0) & (j == 0))
    def _first_fetch():
        x_copy(0).start()

    @pl.when(j == 0)
    def _init():
        x_copy(i).wait()
        x = x_buf[...]
        xb_ref[...] = x.astype(BF16)
        o_ref[...] = alpha * x

    @pl.when((j == 1) & (i + 1 < pl.num_programs(0)))
    def _prefetch():
        x_copy(i + 1).start()

    xb = xb_ref[...]
    gate = jnp.dot(xb, wg_ref[...], preferred_element_type=F32)
    up = jnp.dot(xb, wu_ref[...], preferred_element_type=F32)
    act = (gate * _sigmoid(gate)) * up
    o_ref[...] += jnp.dot(act.astype(BF16), wout_ref[...], preferred_element_type=F32)

    @pl.when(j == pl.num_programs(1) - 1)
    def _finish():
        o_ref[...] = _layer_norm(o_ref[...], g_ref[...], b_ref[...])


def _ffn(x, w_in, w_out, ln_g, ln_b, *, alpha):
    m, d = x.shape
    f = w_out.shape[0]
    tm = min(FFN_TM, m)
    tf = FFN_TF
    assert m % tm == 0 and f % tf == 0
    nf = f // tf
    assert nf >= 2
    vmem = 2 * tm * d * 4 + tm * d * (2 + 4) + 2 * 3 * d * tf * 2 + 3 * tm * tf * 4
    return pl.pallas_call(
        functools.partial(_ffn_kernel, alpha=alpha),
        grid=(m // tm, nf),
        in_specs=[pl.BlockSpec(memory_space=pl.ANY),
                  pl.BlockSpec((d, tf), lambda i, j: (0, j)),
                  pl.BlockSpec((d, tf), lambda i, j: (0, nf + j)),
                  pl.BlockSpec((tf, d), lambda i, j: (j, 0)),
                  pl.BlockSpec((1, d), lambda i, j: (0, 0)),
                  pl.BlockSpec((1, d), lambda i, j: (0, 0))],
        out_specs=pl.BlockSpec((tm, d), lambda i, j: (i, 0)),
        out_shape=jax.ShapeDtypeStruct((m, d), F32),
        scratch_shapes=[pltpu.VMEM((tm, d), F32), pltpu.VMEM((tm, d), BF16), pltpu.SemaphoreType.DMA(())],
        compiler_params=_params(("arbitrary", "arbitrary"), vmem),
        name="ffn",
    )(x, w_in, w_in, w_out, ln_g.reshape(1, d), ln_b.reshape(1, d))


def _regroup_w_in(w_in):
    d = w_in.shape[0]
    splits = np.cumsum([NA_WIDTH, NA_WIDTH, NA_WIDTH, 2 * CONV_WIDTH, MEM_WIDTH, d, d])
    q, k, v, u, qm, g_na, g_conv, g_mem = jnp.split(w_in, splits, axis=1)
    cols = jnp.concatenate([g_na, g_conv, g_mem, q, k, v, u, qm], axis=1).astype(BF16)
    pad = -cols.shape[1] % IN_PROJ_TN
    return jnp.pad(cols, ((0, 0), (0, pad)))


def _encoder_layer(x, mem, b, t, p, *, alpha):
    d = x.shape[1]
    gate_col = 0
    qkv_off = 3 * d
    q_col = qkv_off // V7X_LANES
    k_col = (qkv_off + NA_WIDTH) // V7X_LANES
    v_col = (qkv_off + 2 * NA_WIDTH) // V7X_LANES
    u_off = qkv_off + 3 * NA_WIDTH
    assert u_off % CONV_WIDTH == 0
    a_col = u_off // CONV_WIDTH
    qm_off = u_off + 2 * CONV_WIDTH
    assert qm_off % MEM_HEAD_DIM == 0 and qkv_off % V7X_LANES == 0
    qm_col = qm_off // MEM_HEAD_DIM

    h = _matmul(x, p["w_in"], tm=IN_PROJ_TM, tn=IN_PROJ_TN, name="in_proj")
    h3 = h.reshape(b, t, h.shape[1])
    y_na = _na_attention(h3, p["na_bias"], q_col=q_col, k_col=k_col, v_col=v_col)
    y_conv = _conv_module(h3, p["conv_w"], p["conv_b"], p["conv_ln_g"], p["conv_ln_b"],
                          a_col=a_col, g_col=a_col + 1)
    mt = mem.shape[1]
    kv = _matmul(mem.reshape(b * mt, d), p["w_mem_kv"], tm=b * mt, tn=MEM_KV_TN, name="mem_kv")
    y_mem = _mem_attention(h3, kv.reshape(b, mt, 2 * MEM_WIDTH), q_col=qm_col)
    mixed = _mix(y_na.reshape(b * t, NA_WIDTH), y_conv.reshape(b * t, CONV_WIDTH),
                 y_mem.reshape(b * t, MEM_WIDTH), h, p["w_pa"], p["w_pb"], p["w_pc"], gate_col=gate_col)
    x = _proj_ln(mixed, x, p["w_o"], p["ln1_g"], p["ln1_b"], alpha=alpha)
    return _ffn(x, p["w_ffn_in"], p["w_ffn_out"], p["ln2_g"], p["ln2_b"], alpha=alpha)


def _layer_params(l, w_in, w_mem_kv, rpb, conv_w, conv_b, conv_ln_g, conv_ln_b, w_pa, w_pb, w_pc, w_o,
                  ln1_g, ln1_b, w_ffn_in, w_ffn_out, ln2_g, ln2_b):
    return dict(
        w_in=_regroup_w_in(w_in[l]),
        w_mem_kv=w_mem_kv[l].astype(BF16),
        na_bias=_na_bias_table(rpb[l]),
        conv_w=conv_w[l], conv_b=conv_b[l], conv_ln_g=conv_ln_g[l], conv_ln_b=conv_ln_b[l],
        w_pa=w_pa[l].astype(BF16), w_pb=w_pb[l].astype(BF16), w_pc=w_pc[l].astype(BF16),
        w_o=w_o[l].astype(BF16), ln1_g=ln1_g[l], ln1_b=ln1_b[l],
        w_ffn_in=w_ffn_in[l].astype(BF16), w_ffn_out=w_ffn_out[l].astype(BF16),
        ln2_g=ln2_g[l], ln2_b=ln2_b[l],
    )


def _run_trunk(x, mem, layers, alpha):
    b, t, d = x.shape
    y = x.reshape(b * t, d)
    for p in layers:
        y = _encoder_layer(y, mem, b, t, p, alpha=alpha)
    return y.reshape(b, t, d)


def kernel(x_prompt, x_sample, mem_prompt, mem_sample, w_in, w_mem_kv, rpb, conv_w, conv_b, conv_ln_g,
           conv_ln_b, w_pa, w_pb, w_pc, w_o, ln1_g, ln1_b, w_ffn_in, w_ffn_out, ln2_g, ln2_b):
    depth = w_in.shape[0]
    alpha = (2 * depth) ** 0.25
    weights = (w_in, w_mem_kv, rpb, conv_w, conv_b, conv_ln_g, conv_ln_b, w_pa, w_pb, w_pc, w_o,
               ln1_g, ln1_b, w_ffn_in, w_ffn_out, ln2_g, ln2_b)
    layers = [_layer_params(l, *weights) for l in range(depth)]
    y_prompt = _run_trunk(x_prompt, mem_prompt, layers, alpha)
    y_sample = _run_trunk(x_sample, mem_sample, layers, alpha)
    return (y_prompt, y_sample)
```

```python
import functools

import jax
import jax.numpy as jnp
import numpy as np
from jax import lax
from jax.experimental import pallas as pl
from jax.experimental.pallas import tpu as pltpu

F32 = jnp.float32
BF16 = jnp.bfloat16

GRID_W = 64
NA_HEADS = 12
NA_HEAD_DIM = 64
NA_WIDTH = NA_HEADS * NA_HEAD_DIM
NA_WIN_ROWS = 8
NA_WIN_COLS = 16
RPB_ROWS = 2 * NA_WIN_ROWS - 1
RPB_COLS = 2 * NA_WIN_COLS - 1
CONV_WIDTH = 768
CONV_KERNEL = 31
MEM_HEADS = 4
MEM_HEAD_DIM = 128
MEM_WIDTH = MEM_HEADS * MEM_HEAD_DIM
LN_EPS = 1e-5
MASK_VALUE = -1e30

V7X_LANES = 128
V7X_SUBLANES = 8
V7X_VMEM_BYTES = 64 * 1024 * 1024
V7X_VMEM_RESERVE_BYTES = 8 * 1024 * 1024
KERNEL_TEMP_BYTES = 16 * 1024 * 1024

IN_PROJ_TM = 1024
IN_PROJ_TN = 1536
MEM_KV_TN = 512
NA_ROWS_PER_STEP = 16
NA_SCORE_LOOKAHEAD = 3
CONV_TT = 256
CONV_HALO = 16
CONV_ROW_CHUNK = 64
MEM_TQ = 1024
MIX_TM = 512
MIX_CHUNK = 512
PROJ_TM = 512
FFN_TM = 1024
FFN_TF = 512


def _params(semantics, block_bytes):
    limit = min(block_bytes + KERNEL_TEMP_BYTES, V7X_VMEM_BYTES - V7X_VMEM_RESERVE_BYTES)
    return pltpu.CompilerParams(dimension_semantics=semantics, vmem_limit_bytes=int(limit))


def _sigmoid(x):
    return 1.0 / (1.0 + jnp.exp(-x))


def _layer_norm(z, g, b):
    mu = jnp.mean(z, axis=-1, keepdims=True)
    zc = z - mu
    var = jnp.mean(zc * zc, axis=-1, keepdims=True)
    return zc * lax.rsqrt(var + LN_EPS) * g + b


def _matmul_kernel(x_ref, w_ref, o_ref, xb_ref):
    @pl.when(pl.program_id(1) == 0)
    def _cast():
        xb_ref[...] = x_ref[...].astype(BF16)

    o_ref[...] = jnp.dot(xb_ref[...], w_ref[...], preferred_element_type=F32).astype(o_ref.dtype)


def _matmul(x, w, layer, *, tm, tn, name):
    m, k = x.shape
    n = w.shape[2]
    tm = min(tm, m)
    assert m % tm == 0 and n % tn == 0
    vmem = 2 * tm * k * 4 + tm * k * 2 + 2 * k * tn * 2 + 2 * tm * tn * 2
    return pl.pallas_call(
        _matmul_kernel,
        grid=(m // tm, n // tn),
        in_specs=[pl.BlockSpec((tm, k), lambda i, j: (i, 0)),
                  pl.BlockSpec((None, k, tn), lambda i, j: (layer, 0, j))],
        out_specs=pl.BlockSpec((tm, tn), lambda i, j: (i, j)),
        out_shape=jax.ShapeDtypeStruct((m, n), BF16),
        scratch_shapes=[pltpu.VMEM((tm, k), BF16)],
        compiler_params=_params(("parallel", "arbitrary"), vmem),
        name=name,
    )(x, w)


def _na_bias_table(rpb):
    heads = rpb.shape[0]
    rpb = rpb.astype(F32)
    period = 2 * GRID_W
    zeros = jnp.zeros((heads, RPB_ROWS, period - RPB_COLS), F32)
    e = jnp.concatenate([rpb[..., NA_WIN_COLS - 1:], zeros, rpb[..., :NA_WIN_COLS - 1]], axis=-1)
    tz = jnp.tile(e, (1, 1, GRID_W))[..., :GRID_W * (period - 1)]
    tz = tz.reshape(heads, RPB_ROWS, GRID_W, period - 1)[..., :GRID_W]
    t = jnp.stack([tz[:, NA_WIN_ROWS - 1 - d:2 * NA_WIN_ROWS - 1 - d] for d in range(NA_WIN_ROWS)], axis=1)
    t = t.transpose(0, 1, 3, 2, 4)
    q = np.arange(GRID_W)[:, None]
    c = np.arange(GRID_W)[None, :]
    col_start = np.clip(q - NA_WIN_COLS // 2, 0, GRID_W - NA_WIN_COLS)
    valid = (c >= col_start) & (c < col_start + NA_WIN_COLS)
    t = jnp.where(valid[None, None, :, None, :], t, MASK_VALUE)
    t = t.reshape(heads // 2, 2, NA_WIN_ROWS, GRID_W, NA_WIN_ROWS * GRID_W)
    return t.transpose(0, 2, 1, 3, 4).reshape(heads // 2, NA_WIN_ROWS, 2 * GRID_W, NA_WIN_ROWS * GRID_W)


def _na_kernel(q_ref, k_ref, v_ref, bias_ref, o_ref, *, n_rows):
    rb = pl.program_id(2)
    win = NA_WIN_ROWS * GRID_W
    lane = lax.broadcasted_iota(jnp.int32, (GRID_W, 2 * NA_HEAD_DIM), 1)
    first_head = lane < NA_HEAD_DIM
    scale = NA_HEAD_DIM ** -0.5

    def window_start(i):
        r = rb * NA_ROWS_PER_STEP + i
        start = jnp.clip(r - NA_WIN_ROWS // 2, 0, n_rows - NA_WIN_ROWS)
        return pl.multiple_of(start * GRID_W, GRID_W), r - start

    def scores(i):
        tok, shift = window_start(i)
        q = q_ref[0, i * GRID_W:(i + 1) * GRID_W, :] * scale
        kw = k_ref[0, pl.ds(tok, win), :]
        zero = jnp.zeros_like(q)
        q2 = jnp.concatenate([jnp.where(first_head, q, zero), jnp.where(first_head, zero, q)], axis=0)
        s = lax.dot_general(q2, kw, (((1,), (1,)), ((), ())), preferred_element_type=F32)
        return s + bias_ref[0, shift]

    pending = [scores(i) for i in range(NA_SCORE_LOOKAHEAD)]
    for i in range(NA_ROWS_PER_STEP):
        s = pending.pop(0)
        if i + NA_SCORE_LOOKAHEAD < NA_ROWS_PER_STEP:
            pending.append(scores(i + NA_SCORE_LOOKAHEAD))
        tok, _ = window_start(i)
        vw = v_ref[0, pl.ds(tok, win), :]
        m = jnp.max(s, axis=-1, keepdims=True)
        p = jnp.exp(s - m)
        l = jnp.sum(p, axis=-1, keepdims=True)
        o2 = jnp.dot(p.astype(BF16), vw, preferred_element_type=F32) / l
        o = jnp.where(first_head, o2[:GRID_W], o2[GRID_W:])
        o_ref[0, i * GRID_W:(i + 1) * GRID_W, :] = o.astype(o_ref.dtype)


def _na_attention(h3, bias, layer, *, q_col, k_col, v_col):
    b, t, _ = h3.shape
    n_rows = t // GRID_W
    pairs = NA_HEADS // 2
    tq = NA_ROWS_PER_STEP * GRID_W
    assert n_rows % NA_ROWS_PER_STEP == 0 and n_rows >= NA_WIN_ROWS
    lanes = 2 * NA_HEAD_DIM
    bias_block = (1, NA_WIN_ROWS, 2 * GRID_W, NA_WIN_ROWS * GRID_W)
    bias_bytes = NA_WIN_ROWS * 2 * GRID_W * NA_WIN_ROWS * GRID_W * 4
    vmem = 2 * (2 * tq * lanes * 2 + 2 * t * lanes * 2 + bias_bytes)
    return pl.pallas_call(
        functools.partial(_na_kernel, n_rows=n_rows),
        grid=(pairs, b, n_rows // NA_ROWS_PER_STEP),
        in_specs=[
            pl.BlockSpec((1, tq, lanes), lambda hp, bi, rb: (bi, rb, q_col + hp)),
            pl.BlockSpec((1, t, lanes), lambda hp, bi, rb: (bi, 0, k_col + hp)),
            pl.BlockSpec((1, t, lanes), lambda hp, bi, rb: (bi, 0, v_col + hp)),
            pl.BlockSpec(bias_block, lambda hp, bi, rb: (layer * pairs + hp, 0, 0, 0)),
        ],
        out_specs=pl.BlockSpec((1, tq, lanes), lambda hp, bi, rb: (bi, rb, hp)),
        out_shape=jax.ShapeDtypeStruct((b, t, NA_WIDTH), BF16),
        compiler_params=_params(("parallel", "parallel", "arbitrary"), vmem),
        name="na_attn",
    )(h3, h3, h3, bias)


def _conv_kernel(a_ref, g_ref, ap_ref, gp_ref, an_ref, gn_ref, w_ref, cb_ref, lg_ref, lb_ref, o_ref, u_ref,
                 c_ref, *, n_tiles):
    ti = pl.program_id(1)
    tt = a_ref.shape[1]
    pad = CONV_KERNEL // 2

    def glu(a, g):
        return a.astype(F32) * _sigmoid(g.astype(F32))

    u_ref[CONV_HALO:CONV_HALO + tt, :] = glu(a_ref[0], g_ref[0])
    u_ref[0:CONV_HALO, :] = jnp.where(ti > 0, glu(ap_ref[0], gp_ref[0]), 0.0)
    u_ref[CONV_HALO + tt:2 * CONV_HALO + tt, :] = jnp.where(ti < n_tiles - 1, glu(an_ref[0], gn_ref[0]), 0.0)

    rc = CONV_ROW_CHUNK

    def chunk(ci, carry):
        base = pl.multiple_of(ci * rc, rc)
        for lc in range(CONV_WIDTH // V7X_LANES):
            ls = slice(lc * V7X_LANES, (lc + 1) * V7X_LANES)
            acc = None
            for r in range(V7X_SUBLANES):
                part = None
                for a in range(-2, 2):
                    j = V7X_SUBLANES * a + r + pad
                    if not 0 <= j < CONV_KERNEL:
                        continue
                    rows = pl.ds(pl.multiple_of(base + CONV_HALO + V7X_SUBLANES * a, V7X_SUBLANES),
                                 rc + V7X_SUBLANES)
                    term = u_ref[rows, ls] * w_ref[j:j + 1, ls]
                    part = term if part is None else part + term
                part = part[r:r + rc]
                acc = part if acc is None else acc + part
            c_ref[:, ls] = acc
        y = _layer_norm(c_ref[...] + cb_ref[...], lg_ref[...], lb_ref[...])
        y = y * _sigmoid(y)
        o_ref[0, pl.ds(base, rc), :] = y.astype(o_ref.dtype)
        return carry

    lax.fori_loop(0, tt // rc, chunk, 0)


def _conv_module(h3, conv_w, conv_b, ln_g, ln_b, layer, *, a_col, g_col):
    b, t, _ = h3.shape
    tt = min(CONV_TT, t)
    assert t % tt == 0 and tt % CONV_HALO == 0 and tt % CONV_ROW_CHUNK == 0
    n_tiles = t // tt
    hpt = tt // CONV_HALO
    last_halo = t // CONV_HALO - 1
    cw = CONV_WIDTH

    def cur(col):
        return pl.BlockSpec((1, tt, cw), lambda bi, ti: (bi, ti, col))

    def prev(col):
        return pl.BlockSpec((1, CONV_HALO, cw), lambda bi, ti: (bi, jnp.maximum(ti * hpt - 1, 0), col))

    def nxt(col):
        return pl.BlockSpec((1, CONV_HALO, cw), lambda bi, ti: (bi, jnp.minimum((ti + 1) * hpt, last_halo), col))

    def vec(rows):
        return pl.BlockSpec((None, rows, cw), lambda bi, ti: (layer, 0, 0))

    vmem = 2 * (2 * tt * cw * 2 + 4 * CONV_HALO * cw * 2 + tt * cw * 2) + (tt + 2 * CONV_HALO) * cw * 4 \
        + 2 * (CONV_KERNEL + 3) * cw * 4
    return pl.pallas_call(
        functools.partial(_conv_kernel, n_tiles=n_tiles),
        grid=(b, n_tiles),
        in_specs=[cur(a_col), cur(g_col), prev(a_col), prev(g_col), nxt(a_col), nxt(g_col),
                  vec(CONV_KERNEL), vec(1), vec(1), vec(1)],
        out_specs=pl.BlockSpec((1, tt, cw), lambda bi, ti: (bi, ti, 0)),
        out_shape=jax.ShapeDtypeStruct((b, t, cw), BF16),
        scratch_shapes=[pltpu.VMEM((tt + 2 * CONV_HALO, cw), F32), pltpu.VMEM((CONV_ROW_CHUNK, cw), F32)],
        compiler_params=_params(("parallel", "parallel"), vmem),
        name="conv_mod",
    )(h3, h3, h3, h3, h3, h3, conv_w, conv_b, ln_g, ln_b)


def _mem_attn_kernel(q_ref, k_ref, v_ref, o_ref):
    s = lax.dot_general(q_ref[0], k_ref[0], (((1,), (1,)), ((), ())), preferred_element_type=F32)
    s = s * (MEM_HEAD_DIM ** -0.5)
    m = jnp.max(s, axis=-1, keepdims=True)
    p = jnp.exp(s - m)
    l = jnp.sum(p, axis=-1, keepdims=True)
    o = jnp.dot(p.astype(BF16), v_ref[0], preferred_element_type=F32) / l
    o_ref[0] = o.astype(o_ref.dtype)


def _mem_attention(h3, kv3, *, q_col):
    b, t, _ = h3.shape
    mt = kv3.shape[1]
    tq = min(MEM_TQ, t)
    assert t % tq == 0
    hd = MEM_HEAD_DIM
    vmem = 2 * (2 * tq * hd * 2 + 2 * mt * hd * 2) + 3 * tq * mt * 4
    return pl.pallas_call(
        _mem_attn_kernel,
        grid=(b, t // tq, MEM_HEADS),
        in_specs=[pl.BlockSpec((1, tq, hd), lambda bi, ti, hi: (bi, ti, q_col + hi)),
                  pl.BlockSpec((1, mt, hd), lambda bi, ti, hi: (bi, 0, hi)),
                  pl.BlockSpec((1, mt, hd), lambda bi, ti, hi: (bi, 0, MEM_HEADS + hi))],
        out_specs=pl.BlockSpec((1, tq, hd), lambda bi, ti, hi: (bi, ti, hi)),
        out_shape=jax.ShapeDtypeStruct((b, t, MEM_WIDTH), BF16),
        compiler_params=_params(("parallel", "parallel", "parallel"), vmem),
        name="mem_attn",
    )(h3, kv3, kv3)


def _mix_kernel(yna_ref, yconv_ref, ymem_ref, gna_ref, gconv_ref, gmem_ref, wpa_ref, wpb_ref, wpc_ref, o_ref):
    d = o_ref.shape[1]
    for n in range(d // MIX_CHUNK):
        sl = slice(n * MIX_CHUNK, (n + 1) * MIX_CHUNK)
        ya = jnp.dot(yna_ref[...], wpa_ref[:, sl], preferred_element_type=F32)
        yb = jnp.dot(yconv_ref[...], wpb_ref[:, sl], preferred_element_type=F32)
        yc = jnp.dot(ymem_ref[...], wpc_ref[:, sl], preferred_element_type=F32)
        mixed = (_sigmoid(gna_ref[:, sl].astype(F32)) * ya
                 + _sigmoid(gconv_ref[:, sl].astype(F32)) * yb
                 + _sigmoid(gmem_ref[:, sl].astype(F32)) * yc)
        o_ref[:, sl] = mixed.astype(o_ref.dtype)


def _mix(y_na, y_conv, y_mem, h, w_pa, w_pb, w_pc, layer, *, gate_col):
    m = y_na.shape[0]
    d = w_pa.shape[2]
    tm = min(MIX_TM, m)
    assert m % tm == 0 and d % MIX_CHUNK == 0

    def rows(width, col=0):
        return pl.BlockSpec((tm, width), lambda i: (i, col))

    def whole(w):
        return pl.BlockSpec((None,) + w.shape[1:], lambda i: (layer, 0, 0), pipeline_mode=pl.Buffered(1))

    vmem = 2 * tm * (2 * NA_WIDTH + MEM_WIDTH + 3 * d + d) * 2 + (2 * NA_WIDTH + MEM_WIDTH) * d * 2 \
        + 6 * tm * MIX_CHUNK * 4
    return pl.pallas_call(
        _mix_kernel,
        grid=(m // tm,),
        in_specs=[rows(NA_WIDTH), rows(CONV_WIDTH), rows(MEM_WIDTH),
                  rows(d, gate_col), rows(d, gate_col + 1), rows(d, gate_col + 2),
                  whole(w_pa), whole(w_pb), whole(w_pc)],
        out_specs=rows(d),
        out_shape=jax.ShapeDtypeStruct((m, d), BF16),
        compiler_params=_params(("parallel",), vmem),
        name="mix",
    )(y_na, y_conv, y_mem, h, h, h, w_pa, w_pb, w_pc)


def _proj_ln_kernel(mixed_ref, x_ref, wo_ref, g_ref, b_ref, o_ref, *, alpha):
    z = alpha * x_ref[...] + jnp.dot(mixed_ref[...], wo_ref[...], preferred_element_type=F32)
    o_ref[...] = _layer_norm(z, g_ref[...], b_ref[...])


def _proj_ln(mixed, x, w_o, ln_g, ln_b, layer, *, alpha):
    m, d = x.shape
    tm = min(PROJ_TM, m)
    assert m % tm == 0
    vmem = 2 * tm * d * (2 + 4 + 4) + d * d * 2 + 2 * tm * d * 4
    return pl.pallas_call(
        functools.partial(_proj_ln_kernel, alpha=alpha),
        grid=(m // tm,),
        in_specs=[pl.BlockSpec((tm, d), lambda i: (i, 0)),
                  pl.BlockSpec((tm, d), lambda i: (i, 0)),
                  pl.BlockSpec((None, d, d), lambda i: (layer, 0, 0), pipeline_mode=pl.Buffered(1)),
                  pl.BlockSpec((None, 1, d), lambda i: (layer, 0, 0)),
                  pl.BlockSpec((None, 1, d), lambda i: (layer, 0, 0))],
        out_specs=pl.BlockSpec((tm, d), lambda i: (i, 0)),
        out_shape=jax.ShapeDtypeStruct((m, d), F32),
        compiler_params=_params(("parallel",), vmem),
        name="proj_ln",
    )(mixed, x, w_o, ln_g, ln_b)


def _ffn_kernel(x_hbm, wg_ref, wu_ref, wout_ref, g_ref, b_ref, o_ref, x_buf, xb_ref, x_sem, *, alpha):
    i = pl.program_id(0)
    j = pl.program_id(1)
    tm = x_buf.shape[0]

    def x_copy(tile):
        return pltpu.make_async_copy(x_hbm.at[pl.ds(pl.multiple_of(tile * tm, tm), tm), :], x_buf, x_sem)

    @pl.when((i == 0) & (j == 0))
    def _first_fetch():
        x_copy(0).start()

    @pl.when(j == 0)
    def _init():
        x_copy(i).wait()
        x = x_buf[...]
        xb_ref[...] = x.astype(BF16)
        o_ref[...] = alpha * x

    @pl.when((j == 1) & (i + 1 < pl.num_programs(0)))
    def _prefetch():
        x_copy(i + 1).start()

    xb = xb_ref[...]
    gate = jnp.dot(xb, wg_ref[...], preferred_element_type=F32)
    up = jnp.dot(xb, wu_ref[...], preferred_element_type=F32)
    act = (gate * _sigmoid(gate)) * up
    o_ref[...] += jnp.dot(act.astype(BF16), wout_ref[...], preferred_element_type=F32)

    @pl.when(j == pl.num_programs(1) - 1)
    def _finish():
        o_ref[...] = _layer_norm(o_ref[...], g_ref[...], b_ref[...])


def _ffn(x, w_in, w_out, ln_g, ln_b, layer, *, alpha):
    m, d = x.shape
    f = w_out.shape[1]
    tm = min(FFN_TM, m)
    tf = FFN_TF
    assert m % tm == 0 and f % tf == 0
    nf = f // tf
    assert nf >= 2
    vmem = 2 * tm * d * 4 + tm * d * (2 + 4) + 2 * 3 * d * tf * 2 + 3 * tm * tf * 4
    return pl.pallas_call(
        functools.partial(_ffn_kernel, alpha=alpha),
        grid=(m // tm, nf),
        in_specs=[pl.BlockSpec(memory_space=pl.ANY),
                  pl.BlockSpec((None, d, tf), lambda i, j: (layer, 0, j)),
                  pl.BlockSpec((None, d, tf), lambda i, j: (layer, 0, nf + j)),
                  pl.BlockSpec((None, tf, d), lambda i, j: (layer, j, 0)),
                  pl.BlockSpec((None, 1, d), lambda i, j: (layer, 0, 0)),
                  pl.BlockSpec((None, 1, d), lambda i, j: (layer, 0, 0))],
        out_specs=pl.BlockSpec((tm, d), lambda i, j: (i, 0)),
        out_shape=jax.ShapeDtypeStruct((m, d), F32),
        scratch_shapes=[pltpu.VMEM((tm, d), F32), pltpu.VMEM((tm, d), BF16), pltpu.SemaphoreType.DMA(())],
        compiler_params=_params(("arbitrary", "arbitrary"), vmem),
        name="ffn",
    )(x, w_in, w_in, w_out, ln_g, ln_b)


def _regroup_w_in(w_in):
    d = w_in.shape[1]
    splits = np.cumsum([NA_WIDTH, NA_WIDTH, NA_WIDTH, 2 * CONV_WIDTH, MEM_WIDTH, d, d])
    q, k, v, u, qm, g_na, g_conv, g_mem = jnp.split(w_in, splits, axis=2)
    cols = jnp.concatenate([g_na, g_conv, g_mem, q, k, v, u, qm], axis=2).astype(BF16)
    pad = -cols.shape[2] % IN_PROJ_TN
    return jnp.pad(cols, ((0, 0), (0, 0), (0, pad)))


def _prepare_params(w_in, w_mem_kv, rpb, conv_w, conv_b, conv_ln_g, conv_ln_b, w_pa, w_pb, w_pc, w_o,
                    ln1_g, ln1_b, w_ffn_in, w_ffn_out, ln2_g, ln2_b):
    depth, heads = rpb.shape[:2]

    def row(v):
        return v.reshape(depth, 1, v.shape[1])

    return dict(
        w_in=_regroup_w_in(w_in),
        w_mem_kv=w_mem_kv.astype(BF16),
        na_bias=_na_bias_table(rpb.reshape((depth * heads,) + rpb.shape[2:])),
        conv_w=conv_w, conv_b=row(conv_b), conv_ln_g=row(conv_ln_g), conv_ln_b=row(conv_ln_b),
        w_pa=w_pa.astype(BF16), w_pb=w_pb.astype(BF16), w_pc=w_pc.astype(BF16),
        w_o=w_o.astype(BF16), ln1_g=row(ln1_g), ln1_b=row(ln1_b),
        w_ffn_in=w_ffn_in.astype(BF16), w_ffn_out=w_ffn_out.astype(BF16),
        ln2_g=row(ln2_g), ln2_b=row(ln2_b),
    )


def _encoder_layer(x, mem, b, t, p, layer, *, alpha):
    d = x.shape[1]
    gate_col = 0
    qkv_off = 3 * d
    q_col = qkv_off // V7X_LANES
    k_col = (qkv_off + NA_WIDTH) // V7X_LANES
    v_col = (qkv_off + 2 * NA_WIDTH) // V7X_LANES
    u_off = qkv_off + 3 * NA_WIDTH
    assert u_off % CONV_WIDTH == 0
    a_col = u_off // CONV_WIDTH
    qm_off = u_off + 2 * CONV_WIDTH
    assert qm_off % MEM_HEAD_DIM == 0 and qkv_off % V7X_LANES == 0
    qm_col = qm_off // MEM_HEAD_DIM

    h = _matmul(x, p["w_in"], layer, tm=IN_PROJ_TM, tn=IN_PROJ_TN, name="in_proj")
    h3 = h.reshape(b, t, h.shape[1])
    y_na = _na_attention(h3, p["na_bias"], layer, q_col=q_col, k_col=k_col, v_col=v_col)
    y_conv = _conv_module(h3, p["conv_w"], p["conv_b"], p["conv_ln_g"], p["conv_ln_b"], layer,
                          a_col=a_col, g_col=a_col + 1)
    mt = mem.shape[1]
    kv = _matmul(mem.reshape(b * mt, d), p["w_mem_kv"], layer, tm=b * mt, tn=MEM_KV_TN, name="mem_kv")
    y_mem = _mem_attention(h3, kv.reshape(b, mt, 2 * MEM_WIDTH), q_col=qm_col)
    mixed = _mix(y_na.reshape(b * t, NA_WIDTH), y_conv.reshape(b * t, CONV_WIDTH),
                 y_mem.reshape(b * t, MEM_WIDTH), h, p["w_pa"], p["w_pb"], p["w_pc"], layer, gate_col=gate_col)
    x = _proj_ln(mixed, x, p["w_o"], p["ln1_g"], p["ln1_b"], layer, alpha=alpha)
    return _ffn(x, p["w_ffn_in"], p["w_ffn_out"], p["ln2_g"], p["ln2_b"], layer, alpha=alpha)


def _run_trunk(x, mem, p, depth, alpha):
    b, t, d = x.shape
    y = x.reshape(b * t, d)
    for layer in range(depth):
        y = _encoder_layer(y, mem, b, t, p, layer, alpha=alpha)
    return y.reshape(b, t, d)


def kernel(x_prompt, x_sample, mem_prompt, mem_sample, w_in, w_mem_kv, rpb, conv_w, conv_b, conv_ln_g,
           conv_ln_b, w_pa, w_pb, w_pc, w_o, ln1_g, ln1_b, w_ffn_in, w_ffn_out, ln2_g, ln2_b):
    depth = w_in.shape[0]
    alpha = (2 * depth) ** 0.25
    p = _prepare_params(w_in, w_mem_kv, rpb, conv_w, conv_b, conv_ln_g, conv_ln_b, w_pa, w_pb, w_pc, w_o,
                        ln1_g, ln1_b, w_ffn_in, w_ffn_out, ln2_g, ln2_b)
    y_prompt = _run_trunk(x_prompt, mem_prompt, p, depth, alpha)
    y_sample = _run_trunk(x_sample, mem_sample, p, depth, alpha)
    return (y_prompt, y_sample)
```

```python
import functools

import jax
import jax.numpy as jnp
import numpy as np
from jax import lax
from jax.experimental import pallas as pl
from jax.experimental.pallas import tpu as pltpu

F32 = jnp.float32
BF16 = jnp.bfloat16

GRID_W = 64
NA_HEADS = 12
NA_HEAD_DIM = 64
NA_WIDTH = NA_HEADS * NA_HEAD_DIM
NA_WIN_ROWS = 8
NA_WIN_COLS = 16
RPB_ROWS = 2 * NA_WIN_ROWS - 1
RPB_COLS = 2 * NA_WIN_COLS - 1
CONV_WIDTH = 768
CONV_KERNEL = 31
MEM_HEADS = 4
MEM_HEAD_DIM = 128
MEM_WIDTH = MEM_HEADS * MEM_HEAD_DIM
LN_EPS = 1e-5
MASK_VALUE = -1e30
NEG_LOG2_E = -1.4426950408889634

V7X_LANES = 128
V7X_SUBLANES = 8
V7X_VMEM_BYTES = 64 * 1024 * 1024
V7X_VMEM_RESERVE_BYTES = 8 * 1024 * 1024
KERNEL_TEMP_BYTES = 16 * 1024 * 1024

IN_PROJ_TM = 1024
IN_PROJ_TN = 1536
MEM_KV_TN = 512
NA_ROWS_PER_STEP = 16
NA_SCORE_LOOKAHEAD = 3
CONV_HALO = 16
CONV_ROW_CHUNK = 64
MEM_TQ = 1024
MIX_TM = 512
MIX_CHUNK = 512
PROJ_TM = 512
FFN_TM = 1024
FFN_TF = 512


def _params(semantics, block_bytes):
    limit = min(block_bytes + KERNEL_TEMP_BYTES, V7X_VMEM_BYTES - V7X_VMEM_RESERVE_BYTES)
    return pltpu.CompilerParams(dimension_semantics=semantics, vmem_limit_bytes=int(limit))


def _sigmoid(x):
    return 1.0 / (1.0 + jnp.exp2(x * NEG_LOG2_E))


def _layer_norm(z, g, b):
    mu = jnp.mean(z, axis=-1, keepdims=True)
    zc = z - mu
    var = jnp.mean(zc * zc, axis=-1, keepdims=True)
    return zc * lax.rsqrt(var + LN_EPS) * g + b


def _matmul_kernel(x_ref, w_ref, o_ref, xb_ref):
    @pl.when(pl.program_id(1) == 0)
    def _cast():
        xb_ref[...] = x_ref[...].astype(BF16)

    o_ref[...] = jnp.dot(xb_ref[...], w_ref[...], preferred_element_type=F32).astype(o_ref.dtype)


def _matmul(x, w, layer, *, tm, tn, name):
    m, k = x.shape
    n = w.shape[2]
    tm = min(tm, m)
    assert m % tm == 0 and n % tn == 0
    vmem = 2 * tm * k * 4 + tm * k * 2 + 2 * k * tn * 2 + 2 * tm * tn * 2
    return pl.pallas_call(
        _matmul_kernel,
        grid=(m // tm, n // tn),
        in_specs=[pl.BlockSpec((tm, k), lambda i, j: (i, 0)),
                  pl.BlockSpec((None, k, tn), lambda i, j: (layer, 0, j))],
        out_specs=pl.BlockSpec((tm, tn), lambda i, j: (i, j)),
        out_shape=jax.ShapeDtypeStruct((m, n), BF16),
        scratch_shapes=[pltpu.VMEM((tm, k), BF16)],
        compiler_params=_params(("parallel", "arbitrary"), vmem),
        name=name,
    )(x, w)


def _na_bias_table(rpb):
    q = np.arange(GRID_W)[:, None]
    c = np.arange(GRID_W)[None, :]
    col_start = np.clip(q - NA_WIN_COLS // 2, 0, GRID_W - NA_WIN_COLS)
    valid = (c >= col_start) & (c < col_start + NA_WIN_COLS)
    select = (np.arange(RPB_COLS)[:, None, None] == (c - q + NA_WIN_COLS - 1)[None]) & valid[None]
    t = jnp.einsum("hrk,kqc->hrqc", rpb.astype(F32), jnp.asarray(select, F32),
                   precision=lax.Precision.HIGHEST)
    t = jnp.where(valid[None, None], t, MASK_VALUE)
    return jnp.concatenate([t[:, :-1], t[:, 1:]], axis=-1)


def _na_kernel(q_ref, k_ref, v_ref, bias_ref, o_ref, *, n_rows):
    rb = pl.program_id(2)
    win = NA_WIN_ROWS * GRID_W
    lane = lax.broadcasted_iota(jnp.int32, (GRID_W, 2 * NA_HEAD_DIM), 1)
    first_head = lane < NA_HEAD_DIM
    scale = NA_HEAD_DIM ** -0.5

    def window_start(i):
        r = rb * NA_ROWS_PER_STEP + i
        start = jnp.clip(r - NA_WIN_ROWS // 2, 0, n_rows - NA_WIN_ROWS)
        return pl.multiple_of(start * GRID_W, GRID_W), r - start

    def scores(i):
        tok, shift = window_start(i)
        q = q_ref[0, i * GRID_W:(i + 1) * GRID_W, :] * scale
        kw = k_ref[0, pl.ds(tok, win), :]
        zero = jnp.zeros_like(q)
        q2 = jnp.concatenate([jnp.where(first_head, q, zero), jnp.where(first_head, zero, q)], axis=0)
        s = lax.dot_general(q2, kw, (((1,), (1,)), ((), ())), preferred_element_type=F32)
        first_row = NA_WIN_ROWS - 1 - shift
        bias = jnp.concatenate(
            [jnp.concatenate([bias_ref[hh, first_row + 2 * g] for g in range(NA_WIN_ROWS // 2)], axis=1)
             for hh in range(2)], axis=0)
        return s + bias

    pending = [scores(i) for i in range(NA_SCORE_LOOKAHEAD)]
    for i in range(NA_ROWS_PER_STEP):
        s = pending.pop(0)
        if i + NA_SCORE_LOOKAHEAD < NA_ROWS_PER_STEP:
            pending.append(scores(i + NA_SCORE_LOOKAHEAD))
        tok, _ = window_start(i)
        vw = v_ref[0, pl.ds(tok, win), :]
        m = jnp.max(s, axis=-1, keepdims=True)
        p = jnp.exp(s - m)
        l = jnp.sum(p, axis=-1, keepdims=True)
        o2 = jnp.dot(p.astype(BF16), vw, preferred_element_type=F32) / l
        o = jnp.where(first_head, o2[:GRID_W], o2[GRID_W:])
        o_ref[0, i * GRID_W:(i + 1) * GRID_W, :] = o.astype(o_ref.dtype)


def _na_attention(h3, bias, layer, *, q_col, k_col, v_col):
    b, t, _ = h3.shape
    n_rows = t // GRID_W
    pairs = NA_HEADS // 2
    tq = NA_ROWS_PER_STEP * GRID_W
    assert n_rows % NA_ROWS_PER_STEP == 0 and n_rows >= NA_WIN_ROWS
    lanes = 2 * NA_HEAD_DIM
    bias_block = (2,) + bias.shape[1:]
    bias_bytes = 2 * bias.shape[1] * bias.shape[2] * bias.shape[3] * 4
    vmem = 2 * (2 * tq * lanes * 2 + 2 * t * lanes * 2 + bias_bytes)
    return pl.pallas_call(
        functools.partial(_na_kernel, n_rows=n_rows),
        grid=(pairs, b, n_rows // NA_ROWS_PER_STEP),
        in_specs=[
            pl.BlockSpec((1, tq, lanes), lambda hp, bi, rb: (bi, rb, q_col + hp)),
            pl.BlockSpec((1, t, lanes), lambda hp, bi, rb: (bi, 0, k_col + hp)),
            pl.BlockSpec((1, t, lanes), lambda hp, bi, rb: (bi, 0, v_col + hp)),
            pl.BlockSpec(bias_block, lambda hp, bi, rb: (layer * pairs + hp, 0, 0, 0)),
        ],
        out_specs=pl.BlockSpec((1, tq, lanes), lambda hp, bi, rb: (bi, rb, hp)),
        out_shape=jax.ShapeDtypeStruct((b, t, NA_WIDTH), BF16),
        compiler_params=_params(("parallel", "parallel", "arbitrary"), vmem),
        name="na_attn",
    )(h3, h3, h3, bias)


def _conv_mix_kernel(a_ref, g_ref, ap_ref, gp_ref, an_ref, gn_ref, cw_ref, cb_ref, lg_ref, lb_ref,
                     yna_ref, ymem_ref, gna_ref, gconv_ref, gmem_ref, wpa_ref, wpb_ref, wpc_ref,
                     o_ref, u_ref, c_ref, yconv_ref, part_ref, *, tiles_per_seq):
    ti = pl.program_id(0) % tiles_per_seq
    tm = a_ref.shape[0]
    d = o_ref.shape[1]
    pad = CONV_KERNEL // 2
    rc = CONV_ROW_CHUNK

    def glu(a, g):
        return a.astype(F32) * _sigmoid(g.astype(F32))

    u_ref[CONV_HALO:CONV_HALO + tm, :] = glu(a_ref[...], g_ref[...])
    u_ref[0:CONV_HALO, :] = jnp.where(ti > 0, glu(ap_ref[...], gp_ref[...]), 0.0)
    u_ref[CONV_HALO + tm:2 * CONV_HALO + tm, :] = jnp.where(ti < tiles_per_seq - 1, glu(an_ref[...], gn_ref[...]), 0.0)

    col_chunks = [slice(n * MIX_CHUNK, (n + 1) * MIX_CHUNK) for n in range(d // MIX_CHUNK)]

    def side_projection(sl):
        ya = jnp.dot(yna_ref[...], wpa_ref[:, sl], preferred_element_type=F32)
        yc = jnp.dot(ymem_ref[...], wpc_ref[:, sl], preferred_element_type=F32)
        part_ref[:, sl] = (_sigmoid(gna_ref[:, sl].astype(F32)) * ya
                           + _sigmoid(gmem_ref[:, sl].astype(F32)) * yc)

    def conv_projection(sl):
        yb = jnp.dot(yconv_ref[...], wpb_ref[:, sl], preferred_element_type=F32)
        mixed = part_ref[:, sl] + _sigmoid(gconv_ref[:, sl].astype(F32)) * yb
        o_ref[:, sl] = mixed.astype(o_ref.dtype)

    def conv_taps(base, ls):
        acc = None
        for r in range(V7X_SUBLANES):
            part = None
            for a in range(-2, 2):
                j = V7X_SUBLANES * a + r + pad
                if not 0 <= j < CONV_KERNEL:
                    continue
                row0 = base + CONV_HALO + V7X_SUBLANES * a
                term = u_ref[row0:row0 + rc + V7X_SUBLANES, ls] * cw_ref[j:j + 1, ls]
                part = term if part is None else part + term
            part = part[r:r + rc]
            acc = part if acc is None else acc + part
        c_ref[base:base + rc, ls] = acc

    def conv_norm(base):
        y = _layer_norm(c_ref[base:base + rc, :] + cb_ref[...], lg_ref[...], lb_ref[...])
        y = y * _sigmoid(y)
        yconv_ref[base:base + rc, :] = y.astype(yconv_ref.dtype)

    for sl in col_chunks:
        side_projection(sl)
    for base in range(0, tm, rc):
        for lc in range(CONV_WIDTH // V7X_LANES):
            conv_taps(base, slice(lc * V7X_LANES, (lc + 1) * V7X_LANES))
        conv_norm(base)
    for sl in col_chunks:
        conv_projection(sl)


def _conv_mix(h, y_na, y_mem, conv_w, conv_b, ln_g, ln_b, w_pa, w_pb, w_pc, layer, *, t, a_col, g_col, gate_col):
    m = h.shape[0]
    d = w_pa.shape[2]
    tm = min(MIX_TM, t)
    assert t % tm == 0 and tm % CONV_HALO == 0 and tm % CONV_ROW_CHUNK == 0 and d % MIX_CHUNK == 0
    hpt = tm // CONV_HALO
    last_halo = m // CONV_HALO - 1
    cw = CONV_WIDTH

    def rows(width, col=0):
        return pl.BlockSpec((tm, width), lambda i: (i, col))

    def prev(col):
        return pl.BlockSpec((CONV_HALO, cw), lambda i: (jnp.maximum(i * hpt - 1, 0), col))

    def nxt(col):
        return pl.BlockSpec((CONV_HALO, cw), lambda i: (jnp.minimum((i + 1) * hpt, last_halo), col))

    def vec(nrows):
        return pl.BlockSpec((None, nrows, cw), lambda i: (layer, 0, 0))

    def whole(w):
        return pl.BlockSpec((None,) + w.shape[1:], lambda i: (layer, 0, 0), pipeline_mode=pl.Buffered(1))

    vmem = 2 * tm * (2 * cw + NA_WIDTH + MEM_WIDTH + 3 * d + d) * 2 + (2 * NA_WIDTH + MEM_WIDTH) * d * 2 \
        + (tm + 2 * CONV_HALO) * cw * 4 + tm * cw * (4 + 2) + tm * d * 4
    return pl.pallas_call(
        functools.partial(_conv_mix_kernel, tiles_per_seq=t // tm),
        grid=(m // tm,),
        in_specs=[rows(cw, a_col), rows(cw, g_col), prev(a_col), prev(g_col), nxt(a_col), nxt(g_col),
                  vec(CONV_KERNEL), vec(1), vec(1), vec(1),
                  rows(NA_WIDTH), rows(MEM_WIDTH),
                  rows(d, gate_col), rows(d, gate_col + 1), rows(d, gate_col + 2),
                  whole(w_pa), whole(w_pb), whole(w_pc)],
        out_specs=rows(d),
        out_shape=jax.ShapeDtypeStruct((m, d), BF16),
        scratch_shapes=[pltpu.VMEM((tm + 2 * CONV_HALO, cw), F32), pltpu.VMEM((tm, cw), F32),
                        pltpu.VMEM((tm, cw), BF16), pltpu.VMEM((tm, d), F32)],
        compiler_params=_params(("parallel",), vmem),
        name="conv_mix",
    )(h, h, h, h, h, h, conv_w, conv_b, ln_g, ln_b, y_na, y_mem, h, h, h, w_pa, w_pb, w_pc)


def _mem_attn_kernel(q_ref, k_ref, v_ref, o_ref):
    s = lax.dot_general(q_ref[0], k_ref[0], (((1,), (1,)), ((), ())), preferred_element_type=F32)
    s = s * (MEM_HEAD_DIM ** -0.5)
    m = jnp.max(s, axis=-1, keepdims=True)
    p = jnp.exp(s - m)
    l = jnp.sum(p, axis=-1, keepdims=True)
    o = jnp.dot(p.astype(BF16), v_ref[0], preferred_element_type=F32) / l
    o_ref[0] = o.astype(o_ref.dtype)


def _mem_attention(h3, kv3, *, q_col):
    b, t, _ = h3.shape
    mt = kv3.shape[1]
    tq = min(MEM_TQ, t)
    assert t % tq == 0
    hd = MEM_HEAD_DIM
    vmem = 2 * (2 * tq * hd * 2 + 2 * mt * hd * 2) + 3 * tq * mt * 4
    return pl.pallas_call(
        _mem_attn_kernel,
        grid=(b, t // tq, MEM_HEADS),
        in_specs=[pl.BlockSpec((1, tq, hd), lambda bi, ti, hi: (bi, ti, q_col + hi)),
                  pl.BlockSpec((1, mt, hd), lambda bi, ti, hi: (bi, 0, hi)),
                  pl.BlockSpec((1, mt, hd), lambda bi, ti, hi: (bi, 0, MEM_HEADS + hi))],
        out_specs=pl.BlockSpec((1, tq, hd), lambda bi, ti, hi: (bi, ti, hi)),
        out_shape=jax.ShapeDtypeStruct((b, t, MEM_WIDTH), BF16),
        compiler_params=_params(("parallel", "parallel", "parallel"), vmem),
        name="mem_attn",
    )(h3, kv3, kv3)


def _proj_ln_kernel(mixed_ref, x_ref, wo_ref, g_ref, b_ref, o_ref, *, alpha):
    z = alpha * x_ref[...] + jnp.dot(mixed_ref[...], wo_ref[...], preferred_element_type=F32)
    o_ref[...] = _layer_norm(z, g_ref[...], b_ref[...])


def _proj_ln(mixed, x, w_o, ln_g, ln_b, layer, *, alpha):
    m, d = x.shape
    tm = min(PROJ_TM, m)
    assert m % tm == 0
    vmem = 2 * tm * d * (2 + 4 + 4) + d * d * 2 + 2 * tm * d * 4
    return pl.pallas_call(
        functools.partial(_proj_ln_kernel, alpha=alpha),
        grid=(m // tm,),
        in_specs=[pl.BlockSpec((tm, d), lambda i: (i, 0)),
                  pl.BlockSpec((tm, d), lambda i: (i, 0)),
                  pl.BlockSpec((None, d, d), lambda i: (layer, 0, 0), pipeline_mode=pl.Buffered(1)),
                  pl.BlockSpec((None, 1, d), lambda i: (layer, 0, 0)),
                  pl.BlockSpec((None, 1, d), lambda i: (layer, 0, 0))],
        out_specs=pl.BlockSpec((tm, d), lambda i: (i, 0)),
        out_shape=jax.ShapeDtypeStruct((m, d), F32),
        compiler_params=_params(("parallel",), vmem),
        name="proj_ln",
    )(mixed, x, w_o, ln_g, ln_b)


def _ffn_kernel(x_hbm, wg_ref, wu_ref, wout_ref, g_ref, b_ref, o_ref, x_buf, xb_ref, x_sem, *, alpha):
    i = pl.program_id(0)
    j = pl.program_id(1)
    tm = x_buf.shape[0]

    def x_copy(tile):
        return pltpu.make_async_copy(x_hbm.at[pl.ds(pl.multiple_of(tile * tm, tm), tm), :], x_buf, x_sem)

    @pl.when((i == 0) & (j == 0))
    def _first_fetch():
        x_copy(0).start()

    @pl.when(j == 0)
    def _init():
        x_copy(i).wait()
        x = x_buf[...]
        xb_ref[...] = x.astype(BF16)
        o_ref[...] = alpha * x

    @pl.when((j == 1) & (i + 1 < pl.num_programs(0)))
    def _prefetch():
        x_copy(i + 1).start()

    xb = xb_ref[...]
    gate = jnp.dot(xb, wg_ref[...], preferred_element_type=F32)
    up = jnp.dot(xb, wu_ref[...], preferred_element_type=F32)
    act = (gate * _sigmoid(gate)) * up
    o_ref[...] += jnp.dot(act.astype(BF16), wout_ref[...], preferred_element_type=F32)

    @pl.when(j == pl.num_programs(1) - 1)
    def _finish():
        o_ref[...] = _layer_norm(o_ref[...], g_ref[...], b_ref[...])


def _ffn(x, w_in, w_out, ln_g, ln_b, layer, *, alpha):
    m, d = x.shape
    f = w_out.shape[1]
    tm = min(FFN_TM, m)
    tf = FFN_TF
    assert m % tm == 0 and f % tf == 0
    nf = f // tf
    assert nf >= 2
    vmem = 2 * tm * d * 4 + tm * d * (2 + 4) + 2 * 3 * d * tf * 2 + 3 * tm * tf * 4
    return pl.pallas_call(
        functools.partial(_ffn_kernel, alpha=alpha),
        grid=(m // tm, nf),
        in_specs=[pl.BlockSpec(memory_space=pl.ANY),
                  pl.BlockSpec((None, d, tf), lambda i, j: (layer, 0, j)),
                  pl.BlockSpec((None, d, tf), lambda i, j: (layer, 0, nf + j)),
                  pl.BlockSpec((None, tf, d), lambda i, j: (layer, j, 0)),
                  pl.BlockSpec((None, 1, d), lambda i, j: (layer, 0, 0)),
                  pl.BlockSpec((None, 1, d), lambda i, j: (layer, 0, 0))],
        out_specs=pl.BlockSpec((tm, d), lambda i, j: (i, 0)),
        out_shape=jax.ShapeDtypeStruct((m, d), F32),
        scratch_shapes=[pltpu.VMEM((tm, d), F32), pltpu.VMEM((tm, d), BF16), pltpu.SemaphoreType.DMA(())],
        compiler_params=_params(("arbitrary", "arbitrary"), vmem),
        name="ffn",
    )(x, w_in, w_in, w_out, ln_g, ln_b)


def _regroup_w_in(w_in):
    d = w_in.shape[1]
    splits = np.cumsum([NA_WIDTH, NA_WIDTH, NA_WIDTH, 2 * CONV_WIDTH, MEM_WIDTH, d, d])
    q, k, v, u, qm, g_na, g_conv, g_mem = jnp.split(w_in, splits, axis=2)
    cols = jnp.concatenate([g_na, g_conv, g_mem, q, k, v, u, qm], axis=2).astype(BF16)
    pad = -cols.shape[2] % IN_PROJ_TN
    return jnp.pad(cols, ((0, 0), (0, 0), (0, pad)))


def _prepare_params(w_in, w_mem_kv, rpb, conv_w, conv_b, conv_ln_g, conv_ln_b, w_pa, w_pb, w_pc, w_o,
                    ln1_g, ln1_b, w_ffn_in, w_ffn_out, ln2_g, ln2_b):
    depth, heads = rpb.shape[:2]

    def row(v):
        return v.reshape(depth, 1, v.shape[1])

    return dict(
        w_in=_regroup_w_in(w_in),
        w_mem_kv=w_mem_kv.astype(BF16),
        na_bias=_na_bias_table(rpb.reshape((depth * heads,) + rpb.shape[2:])),
        conv_w=conv_w, conv_b=row(conv_b), conv_ln_g=row(conv_ln_g), conv_ln_b=row(conv_ln_b),
        w_pa=w_pa.astype(BF16), w_pb=w_pb.astype(BF16), w_pc=w_pc.astype(BF16),
        w_o=w_o.astype(BF16), ln1_g=row(ln1_g), ln1_b=row(ln1_b),
        w_ffn_in=w_ffn_in.astype(BF16), w_ffn_out=w_ffn_out.astype(BF16),
        ln2_g=row(ln2_g), ln2_b=row(ln2_b),
    )


def _encoder_layer(x, mem, b, t, p, layer, *, alpha):
    d = x.shape[1]
    gate_col = 0
    qkv_off = 3 * d
    q_col = qkv_off // V7X_LANES
    k_col = (qkv_off + NA_WIDTH) // V7X_LANES
    v_col = (qkv_off + 2 * NA_WIDTH) // V7X_LANES
    u_off = qkv_off + 3 * NA_WIDTH
    assert u_off % CONV_WIDTH == 0
    a_col = u_off // CONV_WIDTH
    qm_off = u_off + 2 * CONV_WIDTH
    assert qm_off % MEM_HEAD_DIM == 0 and qkv_off % V7X_LANES == 0
    qm_col = qm_off // MEM_HEAD_DIM

    h = _matmul(x, p["w_in"], layer, tm=IN_PROJ_TM, tn=IN_PROJ_TN, name="in_proj")
    h3 = h.reshape(b, t, h.shape[1])
    y_na = _na_attention(h3, p["na_bias"], layer, q_col=q_col, k_col=k_col, v_col=v_col)
    mt = mem.shape[1]
    kv = _matmul(mem.reshape(b * mt, d), p["w_mem_kv"], layer, tm=b * mt, tn=MEM_KV_TN, name="mem_kv")
    y_mem = _mem_attention(h3, kv.reshape(b, mt, 2 * MEM_WIDTH), q_col=qm_col)
    mixed = _conv_mix(h, y_na.reshape(b * t, NA_WIDTH), y_mem.reshape(b * t, MEM_WIDTH),
                      p["conv_w"], p["conv_b"], p["conv_ln_g"], p["conv_ln_b"],
                      p["w_pa"], p["w_pb"], p["w_pc"], layer,
                      t=t, a_col=a_col, g_col=a_col + 1, gate_col=gate_col)
    x = _proj_ln(mixed, x, p["w_o"], p["ln1_g"], p["ln1_b"], layer, alpha=alpha)
    return _ffn(x, p["w_ffn_in"], p["w_ffn_out"], p["ln2_g"], p["ln2_b"], layer, alpha=alpha)


def _run_trunk(x, mem, p, depth, alpha):
    b, t, d = x.shape
    y = x.reshape(b * t, d)
    for layer in range(depth):
        y = _encoder_layer(y, mem, b, t, p, layer, alpha=alpha)
    return y.reshape(b, t, d)


def kernel(x_prompt, x_sample, mem_prompt, mem_sample, w_in, w_mem_kv, rpb, conv_w, conv_b, conv_ln_g,
           conv_ln_b, w_pa, w_pb, w_pc, w_o, ln1_g, ln1_b, w_ffn_in, w_ffn_out, ln2_g, ln2_b):
    depth = w_in.shape[0]
    alpha = (2 * depth) ** 0.25
    p = _prepare_params(w_in, w_mem_kv, rpb, conv_w, conv_b, conv_ln_g, conv_ln_b, w_pa, w_pb, w_pc, w_o,
                        ln1_g, ln1_b, w_ffn_in, w_ffn_out, ln2_g, ln2_b)
    y_prompt = _run_trunk(x_prompt, mem_prompt, p, depth, alpha)
    y_sample = _run_trunk(x_sample, mem_sample, p, depth, alpha)
    return (y_prompt, y_sample)
```

```python
import functools

import jax
import jax.numpy as jnp
import numpy as np
from jax import lax
from jax.experimental import pallas as pl
from jax.experimental.pallas import tpu as pltpu

F32 = jnp.float32
BF16 = jnp.bfloat16

GRID_W = 64
NA_HEADS = 12
NA_HEAD_DIM = 64
NA_WIDTH = NA_HEADS * NA_HEAD_DIM
NA_WIN_ROWS = 8
NA_WIN_COLS = 16
RPB_ROWS = 2 * NA_WIN_ROWS - 1
RPB_COLS = 2 * NA_WIN_COLS - 1
CONV_WIDTH = 768
CONV_KERNEL = 31
MEM_HEADS = 4
MEM_HEAD_DIM = 128
MEM_WIDTH = MEM_HEADS * MEM_HEAD_DIM
LN_EPS = 1e-5
MASK_VALUE = -1e30
NEG_LOG2_E = -1.4426950408889634

V7X_LANES = 128
V7X_SUBLANES = 8
V7X_VMEM_BYTES = 64 * 1024 * 1024
V7X_VMEM_RESERVE_BYTES = 8 * 1024 * 1024
KERNEL_TEMP_BYTES = 16 * 1024 * 1024

IN_PROJ_TM = 1024
IN_PROJ_TN = 1536
MEM_KV_TN = 512
NA_ROWS_PER_STEP = 16
NA_SCORE_LOOKAHEAD = 3
CONV_HALO = 16
CONV_ROW_CHUNK = 128
MEM_TQ = 1024
MIX_TM = 512
MIX_CHUNK = 512
PROJ_TM = 512
PROJ_ROW_GROUPS = 4
FFN_TM = 1024
FFN_TF = 512


def _params(semantics, block_bytes):
    limit = min(block_bytes + KERNEL_TEMP_BYTES, V7X_VMEM_BYTES - V7X_VMEM_RESERVE_BYTES)
    return pltpu.CompilerParams(dimension_semantics=semantics, vmem_limit_bytes=int(limit))


def _sigmoid(x):
    return 1.0 / (1.0 + jnp.exp2(x * NEG_LOG2_E))


def _layer_norm(z, g, b):
    mu = jnp.mean(z, axis=-1, keepdims=True)
    zc = z - mu
    var = jnp.mean(zc * zc, axis=-1, keepdims=True)
    return zc * lax.rsqrt(var + LN_EPS) * g + b


def _matmul_kernel(x_ref, w_ref, o_ref, xb_ref):
    @pl.when(pl.program_id(1) == 0)
    def _cast():
        xb_ref[...] = x_ref[...].astype(BF16)

    o_ref[...] = jnp.dot(xb_ref[...], w_ref[...], preferred_element_type=F32).astype(o_ref.dtype)


def _matmul(x, w, layer, *, tm, tn, name):
    m, k = x.shape
    n = w.shape[2]
    tm = min(tm, m)
    assert m % tm == 0 and n % tn == 0
    vmem = 2 * tm * k * 4 + tm * k * 2 + 2 * k * tn * 2 + 2 * tm * tn * 2
    return pl.pallas_call(
        _matmul_kernel,
        grid=(m // tm, n // tn),
        in_specs=[pl.BlockSpec((tm, k), lambda i, j: (i, 0)),
                  pl.BlockSpec((None, k, tn), lambda i, j: (layer, 0, j))],
        out_specs=pl.BlockSpec((tm, tn), lambda i, j: (i, j)),
        out_shape=jax.ShapeDtypeStruct((m, n), BF16),
        scratch_shapes=[pltpu.VMEM((tm, k), BF16)],
        compiler_params=_params(("parallel", "arbitrary"), vmem),
        name=name,
    )(x, w)


def _na_bias_table(rpb):
    q = np.arange(GRID_W)[:, None]
    c = np.arange(GRID_W)[None, :]
    col_start = np.clip(q - NA_WIN_COLS // 2, 0, GRID_W - NA_WIN_COLS)
    valid = (c >= col_start) & (c < col_start + NA_WIN_COLS)
    select = (np.arange(RPB_COLS)[:, None, None] == (c - q + NA_WIN_COLS - 1)[None]) & valid[None]
    t = jnp.einsum("hrk,kqc->hrqc", rpb.astype(F32), jnp.asarray(select, F32),
                   precision=lax.Precision.HIGHEST)
    t = jnp.where(valid[None, None], t, MASK_VALUE)
    return jnp.concatenate([t[:, :-1], t[:, 1:]], axis=-1)


def _na_kernel(q_ref, k_ref, v_ref, bias_ref, o_ref, *, n_rows):
    rb = pl.program_id(2)
    win = NA_WIN_ROWS * GRID_W
    lane = lax.broadcasted_iota(jnp.int32, (GRID_W, 2 * NA_HEAD_DIM), 1)
    first_head = lane < NA_HEAD_DIM
    scale = NA_HEAD_DIM ** -0.5

    def window_start(i):
        r = rb * NA_ROWS_PER_STEP + i
        start = jnp.clip(r - NA_WIN_ROWS // 2, 0, n_rows - NA_WIN_ROWS)
        return pl.multiple_of(start * GRID_W, GRID_W), r - start

    def scores(i):
        tok, shift = window_start(i)
        q = q_ref[0, i * GRID_W:(i + 1) * GRID_W, :] * scale
        kw = k_ref[0, pl.ds(tok, win), :]
        zero = jnp.zeros_like(q)
        q2 = jnp.concatenate([jnp.where(first_head, q, zero), jnp.where(first_head, zero, q)], axis=0)
        s = lax.dot_general(q2, kw, (((1,), (1,)), ((), ())), preferred_element_type=F32)
        first_row = NA_WIN_ROWS - 1 - shift
        bias = jnp.concatenate(
            [jnp.concatenate([bias_ref[hh, first_row + 2 * g] for g in range(NA_WIN_ROWS // 2)], axis=1)
             for hh in range(2)], axis=0)
        return s + bias

    pending = [scores(i) for i in range(NA_SCORE_LOOKAHEAD)]
    for i in range(NA_ROWS_PER_STEP):
        s = pending.pop(0)
        if i + NA_SCORE_LOOKAHEAD < NA_ROWS_PER_STEP:
            pending.append(scores(i + NA_SCORE_LOOKAHEAD))
        tok, _ = window_start(i)
        vw = v_ref[0, pl.ds(tok, win), :]
        m = jnp.max(s, axis=-1, keepdims=True)
        p = jnp.exp(s - m)
        l = jnp.sum(p, axis=-1, keepdims=True)
        o2 = jnp.dot(p.astype(BF16), vw, preferred_element_type=F32) / l
        o = jnp.where(first_head, o2[:GRID_W], o2[GRID_W:])
        o_ref[0, i * GRID_W:(i + 1) * GRID_W, :] = o.astype(o_ref.dtype)


def _na_attention(h3, bias, layer, *, q_col, k_col, v_col):
    b, t, _ = h3.shape
    n_rows = t // GRID_W
    pairs = NA_HEADS // 2
    tq = NA_ROWS_PER_STEP * GRID_W
    assert n_rows % NA_ROWS_PER_STEP == 0 and n_rows >= NA_WIN_ROWS
    lanes = 2 * NA_HEAD_DIM
    bias_block = (2,) + bias.shape[1:]
    bias_bytes = 2 * bias.shape[1] * bias.shape[2] * bias.shape[3] * 4
    vmem = 2 * (2 * tq * lanes * 2 + 2 * t * lanes * 2 + bias_bytes)
    return pl.pallas_call(
        functools.partial(_na_kernel, n_rows=n_rows),
        grid=(pairs, b, n_rows // NA_ROWS_PER_STEP),
        in_specs=[
            pl.BlockSpec((1, tq, lanes), lambda hp, bi, rb: (bi, rb, q_col + hp)),
            pl.BlockSpec((1, t, lanes), lambda hp, bi, rb: (bi, 0, k_col + hp)),
            pl.BlockSpec((1, t, lanes), lambda hp, bi, rb: (bi, 0, v_col + hp)),
            pl.BlockSpec(bias_block, lambda hp, bi, rb: (layer * pairs + hp, 0, 0, 0)),
        ],
        out_specs=pl.BlockSpec((1, tq, lanes), lambda hp, bi, rb: (bi, rb, hp)),
        out_shape=jax.ShapeDtypeStruct((b, t, NA_WIDTH), BF16),
        compiler_params=_params(("parallel", "parallel", "arbitrary"), vmem),
        name="na_attn",
    )(h3, h3, h3, bias)


def _conv_mix_kernel(a_ref, g_ref, ap_ref, gp_ref, an_ref, gn_ref, cw_ref, cb_ref, lg_ref, lb_ref,
                     yna_ref, ymem_ref, gna_ref, gconv_ref, gmem_ref, wpa_ref, wpb_ref, wpc_ref,
                     o_ref, u_ref, c_ref, yconv_ref, part_ref, *, tiles_per_seq):
    ti = pl.program_id(0) % tiles_per_seq
    tm = a_ref.shape[0]
    d = o_ref.shape[1]
    pad = CONV_KERNEL // 2
    rc = CONV_ROW_CHUNK

    def glu(a, g):
        return a.astype(F32) * _sigmoid(g.astype(F32))

    u_ref[CONV_HALO:CONV_HALO + tm, :] = glu(a_ref[...], g_ref[...])
    u_ref[0:CONV_HALO, :] = jnp.where(ti > 0, glu(ap_ref[...], gp_ref[...]), 0.0)
    u_ref[CONV_HALO + tm:2 * CONV_HALO + tm, :] = jnp.where(ti < tiles_per_seq - 1, glu(an_ref[...], gn_ref[...]), 0.0)

    col_chunks = [slice(n * MIX_CHUNK, (n + 1) * MIX_CHUNK) for n in range(d // MIX_CHUNK)]

    def side_projection(sl):
        ya = jnp.dot(yna_ref[...], wpa_ref[:, sl], preferred_element_type=F32)
        yc = jnp.dot(ymem_ref[...], wpc_ref[:, sl], preferred_element_type=F32)
        part_ref[:, sl] = (_sigmoid(gna_ref[:, sl].astype(F32)) * ya
                           + _sigmoid(gmem_ref[:, sl].astype(F32)) * yc)

    def conv_projection(sl):
        yb = jnp.dot(yconv_ref[...], wpb_ref[:, sl], preferred_element_type=F32)
        mixed = part_ref[:, sl] + _sigmoid(gconv_ref[:, sl].astype(F32)) * yb
        o_ref[:, sl] = mixed.astype(o_ref.dtype)

    def conv_taps(base, ls):
        acc = None
        for r in range(V7X_SUBLANES):
            part = None
            for a in range(-2, 2):
                j = V7X_SUBLANES * a + r + pad
                if not 0 <= j < CONV_KERNEL:
                    continue
                row0 = base + CONV_HALO + V7X_SUBLANES * a
                term = u_ref[row0:row0 + rc + V7X_SUBLANES, ls] * cw_ref[j:j + 1, ls]
                part = term if part is None else part + term
            part = part[r:r + rc]
            acc = part if acc is None else acc + part
        c_ref[base:base + rc, ls] = acc

    def conv_norm(base):
        y = _layer_norm(c_ref[base:base + rc, :] + cb_ref[...], lg_ref[...], lb_ref[...])
        y = y * _sigmoid(y)
        yconv_ref[base:base + rc, :] = y.astype(yconv_ref.dtype)

    for sl in col_chunks:
        side_projection(sl)
    for base in range(0, tm, rc):
        for lc in range(CONV_WIDTH // V7X_LANES):
            conv_taps(base, slice(lc * V7X_LANES, (lc + 1) * V7X_LANES))
        conv_norm(base)
    for sl in col_chunks:
        conv_projection(sl)


def _conv_mix(h, y_na, y_mem, conv_w, conv_b, ln_g, ln_b, w_pa, w_pb, w_pc, layer, *, t, a_col, g_col, gate_col):
    m = h.shape[0]
    d = w_pa.shape[2]
    tm = min(MIX_TM, t)
    assert t % tm == 0 and tm % CONV_HALO == 0 and tm % CONV_ROW_CHUNK == 0 and d % MIX_CHUNK == 0
    hpt = tm // CONV_HALO
    last_halo = m // CONV_HALO - 1
    cw = CONV_WIDTH

    def rows(width, col=0):
        return pl.BlockSpec((tm, width), lambda i: (i, col))

    def prev(col):
        return pl.BlockSpec((CONV_HALO, cw), lambda i: (jnp.maximum(i * hpt - 1, 0), col))

    def nxt(col):
        return pl.BlockSpec((CONV_HALO, cw), lambda i: (jnp.minimum((i + 1) * hpt, last_halo), col))

    def vec(nrows):
        return pl.BlockSpec((None, nrows, cw), lambda i: (layer, 0, 0))

    def whole(w):
        return pl.BlockSpec((None,) + w.shape[1:], lambda i: (layer, 0, 0), pipeline_mode=pl.Buffered(1))

    vmem = 2 * tm * (2 * cw + NA_WIDTH + MEM_WIDTH + 3 * d + d) * 2 + (2 * NA_WIDTH + MEM_WIDTH) * d * 2 \
        + (tm + 2 * CONV_HALO) * cw * 4 + tm * cw * (4 + 2) + tm * d * 4
    return pl.pallas_call(
        functools.partial(_conv_mix_kernel, tiles_per_seq=t // tm),
        grid=(m // tm,),
        in_specs=[rows(cw, a_col), rows(cw, g_col), prev(a_col), prev(g_col), nxt(a_col), nxt(g_col),
                  vec(CONV_KERNEL), vec(1), vec(1), vec(1),
                  rows(NA_WIDTH), rows(MEM_WIDTH),
                  rows(d, gate_col), rows(d, gate_col + 1), rows(d, gate_col + 2),
                  whole(w_pa), whole(w_pb), whole(w_pc)],
        out_specs=rows(d),
        out_shape=jax.ShapeDtypeStruct((m, d), BF16),
        scratch_shapes=[pltpu.VMEM((tm + 2 * CONV_HALO, cw), F32), pltpu.VMEM((tm, cw), F32),
                        pltpu.VMEM((tm, cw), BF16), pltpu.VMEM((tm, d), F32)],
        compiler_params=_params(("parallel",), vmem),
        name="conv_mix",
    )(h, h, h, h, h, h, conv_w, conv_b, ln_g, ln_b, y_na, y_mem, h, h, h, w_pa, w_pb, w_pc)


def _mem_attn_kernel(*refs):
    q_refs, kv_ref, o_ref = refs[:MEM_HEADS], refs[MEM_HEADS], refs[MEM_HEADS + 1]
    hd = MEM_HEAD_DIM

    def scores(h):
        k = kv_ref[0, :, h * hd:(h + 1) * hd]
        s = lax.dot_general(q_refs[h][0], k, (((1,), (1,)), ((), ())), preferred_element_type=F32)
        return s * (hd ** -0.5)

    pending = [scores(h) for h in range(MEM_HEADS)]
    for h, s in enumerate(pending):
        v = kv_ref[0, :, MEM_WIDTH + h * hd:MEM_WIDTH + (h + 1) * hd]
        m = jnp.max(s, axis=-1, keepdims=True)
        p = jnp.exp(s - m)
        l = jnp.sum(p, axis=-1, keepdims=True)
        o = jnp.dot(p.astype(BF16), v, preferred_element_type=F32) / l
        o_ref[0, :, h * hd:(h + 1) * hd] = o.astype(o_ref.dtype)


def _mem_attention(h3, kv3, *, q_col):
    b, t, _ = h3.shape
    mt = kv3.shape[1]
    tq = min(MEM_TQ, t)
    assert t % tq == 0
    hd = MEM_HEAD_DIM
    vmem = 2 * (2 * tq * MEM_WIDTH * 2 + mt * 2 * MEM_WIDTH * 2) + 2 * MEM_HEADS * tq * mt * 4

    def q_spec(head):
        return pl.BlockSpec((1, tq, hd), lambda bi, ti: (bi, ti, q_col + head))

    return pl.pallas_call(
        _mem_attn_kernel,
        grid=(b, t // tq),
        in_specs=[q_spec(head) for head in range(MEM_HEADS)]
        + [pl.BlockSpec((1, mt, 2 * MEM_WIDTH), lambda bi, ti: (bi, 0, 0))],
        out_specs=pl.BlockSpec((1, tq, MEM_WIDTH), lambda bi, ti: (bi, ti, 0)),
        out_shape=jax.ShapeDtypeStruct((b, t, MEM_WIDTH), BF16),
        compiler_params=_params(("parallel", "parallel"), vmem),
        name="mem_attn",
    )(*([h3] * MEM_HEADS), kv3)


def _proj_ln_kernel(mixed_ref, x_ref, wo_ref, g_ref, b_ref, o_ref, *, alpha):
    tm = x_ref.shape[0]
    for k in range(PROJ_ROW_GROUPS):
        rs = slice(k * tm // PROJ_ROW_GROUPS, (k + 1) * tm // PROJ_ROW_GROUPS)
        z = alpha * x_ref[rs, :] + jnp.dot(mixed_ref[rs, :], wo_ref[...], preferred_element_type=F32)
        o_ref[rs, :] = _layer_norm(z, g_ref[...], b_ref[...])


def _proj_ln(mixed, x, w_o, ln_g, ln_b, layer, *, alpha):
    m, d = x.shape
    tm = min(PROJ_TM, m)
    assert m % tm == 0
    vmem = 2 * tm * d * (2 + 4 + 4) + d * d * 2 + 2 * tm * d * 4
    return pl.pallas_call(
        functools.partial(_proj_ln_kernel, alpha=alpha),
        grid=(m // tm,),
        in_specs=[pl.BlockSpec((tm, d), lambda i: (i, 0)),
                  pl.BlockSpec((tm, d), lambda i: (i, 0)),
                  pl.BlockSpec((None, d, d), lambda i: (layer, 0, 0), pipeline_mode=pl.Buffered(1)),
                  pl.BlockSpec((None, 1, d), lambda i: (layer, 0, 0)),
                  pl.BlockSpec((None, 1, d), lambda i: (layer, 0, 0))],
        out_specs=pl.BlockSpec((tm, d), lambda i: (i, 0)),
        out_shape=jax.ShapeDtypeStruct((m, d), F32),
        compiler_params=_params(("parallel",), vmem),
        name="proj_ln",
    )(mixed, x, w_o, ln_g, ln_b)


def _ffn_kernel(x_hbm, wg_ref, wu_ref, wout_ref, g_ref, b_ref, o_ref, x_buf, xb_ref, x_sem, *, alpha):
    i = pl.program_id(0)
    j = pl.program_id(1)
    tm = x_buf.shape[0]

    def x_copy(tile):
        return pltpu.make_async_copy(x_hbm.at[pl.ds(pl.multiple_of(tile * tm, tm), tm), :], x_buf, x_sem)

    @pl.when((i == 0) & (j == 0))
    def _first_fetch():
        x_copy(0).start()

    @pl.when(j == 0)
    def _init():
        x_copy(i).wait()
        x = x_buf[...]
        xb_ref[...] = x.astype(BF16)
        o_ref[...] = alpha * x

    @pl.when((j == 1) & (i + 1 < pl.num_programs(0)))
    def _prefetch():
        x_copy(i + 1).start()

    xb = xb_ref[...]
    gate = jnp.dot(xb, wg_ref[...], preferred_element_type=F32)
    up = jnp.dot(xb, wu_ref[...], preferred_element_type=F32)
    act = (gate * _sigmoid(gate)) * up
    o_ref[...] += jnp.dot(act.astype(BF16), wout_ref[...], preferred_element_type=F32)

    @pl.when(j == pl.num_programs(1) - 1)
    def _finish():
        o_ref[...] = _layer_norm(o_ref[...], g_ref[...], b_ref[...])


def _ffn(x, w_in, w_out, ln_g, ln_b, layer, *, alpha):
    m, d = x.shape
    f = w_out.shape[1]
    tm = min(FFN_TM, m)
    tf = FFN_TF
    assert m % tm == 0 and f % tf == 0
    nf = f // tf
    assert nf >= 2
    vmem = 2 * tm * d * 4 + tm * d * (2 + 4) + 2 * 3 * d * tf * 2 + 3 * tm * tf * 4
    return pl.pallas_call(
        functools.partial(_ffn_kernel, alpha=alpha),
        grid=(m // tm, nf),
        in_specs=[pl.BlockSpec(memory_space=pl.ANY),
                  pl.BlockSpec((None, d, tf), lambda i, j: (layer, 0, j)),
                  pl.BlockSpec((None, d, tf), lambda i, j: (layer, 0, nf + j)),
                  pl.BlockSpec((None, tf, d), lambda i, j: (layer, j, 0)),
                  pl.BlockSpec((None, 1, d), lambda i, j: (layer, 0, 0)),
                  pl.BlockSpec((None, 1, d), lambda i, j: (layer, 0, 0))],
        out_specs=pl.BlockSpec((tm, d), lambda i, j: (i, 0)),
        out_shape=jax.ShapeDtypeStruct((m, d), F32),
        scratch_shapes=[pltpu.VMEM((tm, d), F32), pltpu.VMEM((tm, d), BF16), pltpu.SemaphoreType.DMA(())],
        compiler_params=_params(("arbitrary", "arbitrary"), vmem),
        name="ffn",
    )(x, w_in, w_in, w_out, ln_g, ln_b)


def _regroup_w_in(w_in):
    d = w_in.shape[1]
    splits = np.cumsum([NA_WIDTH, NA_WIDTH, NA_WIDTH, 2 * CONV_WIDTH, MEM_WIDTH, d, d])
    q, k, v, u, qm, g_na, g_conv, g_mem = jnp.split(w_in, splits, axis=2)
    cols = jnp.concatenate([g_na, g_conv, g_mem, q, k, v, u, qm], axis=2).astype(BF16)
    pad = -cols.shape[2] % IN_PROJ_TN
    return jnp.pad(cols, ((0, 0), (0, 0), (0, pad)))


def _prepare_params(w_in, w_mem_kv, rpb, conv_w, conv_b, conv_ln_g, conv_ln_b, w_pa, w_pb, w_pc, w_o,
                    ln1_g, ln1_b, w_ffn_in, w_ffn_out, ln2_g, ln2_b):
    depth, heads = rpb.shape[:2]

    def row(v):
        return v.reshape(depth, 1, v.shape[1])

    return dict(
        w_in=_regroup_w_in(w_in),
        w_mem_kv=w_mem_kv.astype(BF16),
        na_bias=_na_bias_table(rpb.reshape((depth * heads,) + rpb.shape[2:])),
        conv_w=conv_w, conv_b=row(conv_b), conv_ln_g=row(conv_ln_g), conv_ln_b=row(conv_ln_b),
        w_pa=w_pa.astype(BF16), w_pb=w_pb.astype(BF16), w_pc=w_pc.astype(BF16),
        w_o=w_o.astype(BF16), ln1_g=row(ln1_g), ln1_b=row(ln1_b),
        w_ffn_in=w_ffn_in.astype(BF16), w_ffn_out=w_ffn_out.astype(BF16),
        ln2_g=row(ln2_g), ln2_b=row(ln2_b),
    )


def _encoder_layer(x, mem, b, t, p, layer, *, alpha):
    d = x.shape[1]
    gate_col = 0
    qkv_off = 3 * d
    q_col = qkv_off // V7X_LANES
    k_col = (qkv_off + NA_WIDTH) // V7X_LANES
    v_col = (qkv_off + 2 * NA_WIDTH) // V7X_LANES
    u_off = qkv_off + 3 * NA_WIDTH
    assert u_off % CONV_WIDTH == 0
    a_col = u_off // CONV_WIDTH
    qm_off = u_off + 2 * CONV_WIDTH
    assert qm_off % MEM_HEAD_DIM == 0 and qkv_off % V7X_LANES == 0
    qm_col = qm_off // MEM_HEAD_DIM

    h = _matmul(x, p["w_in"], layer, tm=IN_PROJ_TM, tn=IN_PROJ_TN, name="in_proj")
    h3 = h.reshape(b, t, h.shape[1])
    y_na = _na_attention(h3, p["na_bias"], layer, q_col=q_col, k_col=k_col, v_col=v_col)
    mt = mem.shape[1]
    kv = _matmul(mem.reshape(b * mt, d), p["w_mem_kv"], layer, tm=b * mt, tn=MEM_KV_TN, name="mem_kv")
    y_mem = _mem_attention(h3, kv.reshape(b, mt, 2 * MEM_WIDTH), q_col=qm_col)
    mixed = _conv_mix(h, y_na.reshape(b * t, NA_WIDTH), y_mem.reshape(b * t, MEM_WIDTH),
                      p["conv_w"], p["conv_b"], p["conv_ln_g"], p["conv_ln_b"],
                      p["w_pa"], p["w_pb"], p["w_pc"], layer,
                      t=t, a_col=a_col, g_col=a_col + 1, gate_col=gate_col)
    x = _proj_ln(mixed, x, p["w_o"], p["ln1_g"], p["ln1_b"], layer, alpha=alpha)
    return _ffn(x, p["w_ffn_in"], p["w_ffn_out"], p["ln2_g"], p["ln2_b"], layer, alpha=alpha)


def _run_trunk(x, mem, p, depth, alpha):
    b, t, d = x.shape
    y = x.reshape(b * t, d)
    for layer in range(depth):
        y = _encoder_layer(y, mem, b, t, p, layer, alpha=alpha)
    return y.reshape(b, t, d)


def kernel(x_prompt, x_sample, mem_prompt, mem_sample, w_in, w_mem_kv, rpb, conv_w, conv_b, conv_ln_g,
           conv_ln_b, w_pa, w_pb, w_pc, w_o, ln1_g, ln1_b, w_ffn_in, w_ffn_out, ln2_g, ln2_b):
    depth = w_in.shape[0]
    alpha = (2 * depth) ** 0.25
    p = _prepare_params(w_in, w_mem_kv, rpb, conv_w, conv_b, conv_ln_g, conv_ln_b, w_pa, w_pb, w_pc, w_o,
                        ln1_g, ln1_b, w_ffn_in, w_ffn_out, ln2_g, ln2_b)
    y_prompt = _run_trunk(x_prompt, mem_prompt, p, depth, alpha)
    y_sample = _run_trunk(x_sample, mem_sample, p, depth, alpha)
    return (y_prompt, y_sample)
```

```python
import functools

import jax
import jax.numpy as jnp
import numpy as np
from jax import lax
from jax.experimental import pallas as pl
from jax.experimental.pallas import tpu as pltpu

F32 = jnp.float32
BF16 = jnp.bfloat16

GRID_W = 64
NA_HEADS = 12
NA_HEAD_DIM = 64
NA_WIDTH = NA_HEADS * NA_HEAD_DIM
NA_WIN_ROWS = 8
NA_WIN_COLS = 16
RPB_ROWS = 2 * NA_WIN_ROWS - 1
RPB_COLS = 2 * NA_WIN_COLS - 1
CONV_WIDTH = 768
CONV_KERNEL = 31
MEM_HEADS = 4
MEM_HEAD_DIM = 128
MEM_WIDTH = MEM_HEADS * MEM_HEAD_DIM
LN_EPS = 1e-5
MASK_VALUE = -1e30
NEG_LOG2_E = -1.4426950408889634

V7X_LANES = 128
V7X_SUBLANES = 8
V7X_VMEM_BYTES = 64 * 1024 * 1024
V7X_VMEM_RESERVE_BYTES = 8 * 1024 * 1024
KERNEL_TEMP_BYTES = 16 * 1024 * 1024

IN_PROJ_TM = 1024
IN_PROJ_TN = 1536
MEM_KV_TN = 512
NA_ROWS_PER_STEP = 32
NA_SCORE_LOOKAHEAD = 3
CONV_HALO = 16
CONV_ROW_CHUNK = 128
MEM_TQ = 1024
MIX_TM = 512
MIX_CHUNK = 512
PROJ_TM = 512
PROJ_ROW_GROUPS = 4
FFN_TM = 1024
FFN_TF = 512


def _params(semantics, block_bytes):
    limit = min(block_bytes + KERNEL_TEMP_BYTES, V7X_VMEM_BYTES - V7X_VMEM_RESERVE_BYTES)
    return pltpu.CompilerParams(dimension_semantics=semantics, vmem_limit_bytes=int(limit))


def _sigmoid(x):
    return 1.0 / (1.0 + jnp.exp2(x * NEG_LOG2_E))


def _layer_norm(z, g, b):
    mu = jnp.mean(z, axis=-1, keepdims=True)
    zc = z - mu
    var = jnp.mean(zc * zc, axis=-1, keepdims=True)
    return zc * lax.rsqrt(var + LN_EPS) * g + b


def _matmul_kernel(x_ref, w_ref, o_ref, xb_ref):
    @pl.when(pl.program_id(1) == 0)
    def _cast():
        xb_ref[...] = x_ref[...].astype(BF16)

    o_ref[...] = jnp.dot(xb_ref[...], w_ref[...], preferred_element_type=F32).astype(o_ref.dtype)


def _matmul(x, w, layer, *, tm, tn, name):
    m, k = x.shape
    n = w.shape[2]
    tm = min(tm, m)
    assert m % tm == 0 and n % tn == 0
    vmem = 2 * tm * k * 4 + tm * k * 2 + 2 * k * tn * 2 + 2 * tm * tn * 2
    return pl.pallas_call(
        _matmul_kernel,
        grid=(m // tm, n // tn),
        in_specs=[pl.BlockSpec((tm, k), lambda i, j: (i, 0)),
                  pl.BlockSpec((None, k, tn), lambda i, j: (layer, 0, j))],
        out_specs=pl.BlockSpec((tm, tn), lambda i, j: (i, j)),
        out_shape=jax.ShapeDtypeStruct((m, n), BF16),
        scratch_shapes=[pltpu.VMEM((tm, k), BF16)],
        compiler_params=_params(("parallel", "arbitrary"), vmem),
        name=name,
    )(x, w)


def _na_bias_table(rpb):
    q = np.arange(GRID_W)[:, None]
    c = np.arange(GRID_W)[None, :]
    col_start = np.clip(q - NA_WIN_COLS // 2, 0, GRID_W - NA_WIN_COLS)
    valid = (c >= col_start) & (c < col_start + NA_WIN_COLS)
    select = (np.arange(RPB_COLS)[:, None, None] == (c - q + NA_WIN_COLS - 1)[None]) & valid[None]
    t = jnp.einsum("hrk,kqc->hrqc", rpb.astype(F32), jnp.asarray(select, F32),
                   precision=lax.Precision.HIGHEST)
    t = jnp.where(valid[None, None], t, MASK_VALUE)
    return jnp.concatenate([t[:, :-1], t[:, 1:]], axis=-1)


def _na_kernel(q_ref, k_ref, v_ref, bias_ref, o_ref, *, n_rows):
    rb = pl.program_id(2)
    win = NA_WIN_ROWS * GRID_W
    lane = lax.broadcasted_iota(jnp.int32, (GRID_W, 2 * NA_HEAD_DIM), 1)
    first_head = lane < NA_HEAD_DIM
    scale = NA_HEAD_DIM ** -0.5

    def window_start(i):
        r = rb * NA_ROWS_PER_STEP + i
        start = jnp.clip(r - NA_WIN_ROWS // 2, 0, n_rows - NA_WIN_ROWS)
        return pl.multiple_of(start * GRID_W, GRID_W), r - start

    def scores(i):
        tok, shift = window_start(i)
        q = q_ref[0, i * GRID_W:(i + 1) * GRID_W, :] * scale
        kw = k_ref[0, pl.ds(tok, win), :]
        zero = jnp.zeros_like(q)
        q2 = jnp.concatenate([jnp.where(first_head, q, zero), jnp.where(first_head, zero, q)], axis=0)
        s = lax.dot_general(q2, kw, (((1,), (1,)), ((), ())), preferred_element_type=F32)
        first_row = NA_WIN_ROWS - 1 - shift
        bias = jnp.concatenate(
            [jnp.concatenate([bias_ref[hh, first_row + 2 * g] for g in range(NA_WIN_ROWS // 2)], axis=1)
             for hh in range(2)], axis=0)
        return s + bias

    pending = [scores(i) for i in range(NA_SCORE_LOOKAHEAD)]
    for i in range(NA_ROWS_PER_STEP):
        s = pending.pop(0)
        if i + NA_SCORE_LOOKAHEAD < NA_ROWS_PER_STEP:
            pending.append(scores(i + NA_SCORE_LOOKAHEAD))
        tok, _ = window_start(i)
        vw = v_ref[0, pl.ds(tok, win), :]
        m = jnp.max(s, axis=-1, keepdims=True)
        p = jnp.exp(s - m)
        l = jnp.sum(p, axis=-1, keepdims=True)
        o2 = jnp.dot(p.astype(BF16), vw, preferred_element_type=F32) / l
        o = jnp.where(first_head, o2[:GRID_W], o2[GRID_W:])
        o_ref[0, i * GRID_W:(i + 1) * GRID_W, :] = o.astype(o_ref.dtype)


def _na_attention(h3, bias, layer, *, q_col, k_col, v_col):
    b, t, _ = h3.shape
    n_rows = t // GRID_W
    pairs = NA_HEADS // 2
    tq = NA_ROWS_PER_STEP * GRID_W
    assert n_rows % NA_ROWS_PER_STEP == 0 and n_rows >= NA_WIN_ROWS
    lanes = 2 * NA_HEAD_DIM
    bias_block = (2,) + bias.shape[1:]
    bias_bytes = 2 * bias.shape[1] * bias.shape[2] * bias.shape[3] * 4
    vmem = 2 * (2 * tq * lanes * 2 + 2 * t * lanes * 2 + bias_bytes)
    return pl.pallas_call(
        functools.partial(_na_kernel, n_rows=n_rows),
        grid=(pairs, b, n_rows // NA_ROWS_PER_STEP),
        in_specs=[
            pl.BlockSpec((1, tq, lanes), lambda hp, bi, rb: (bi, rb, q_col + hp)),
            pl.BlockSpec((1, t, lanes), lambda hp, bi, rb: (bi, 0, k_col + hp)),
            pl.BlockSpec((1, t, lanes), lambda hp, bi, rb: (bi, 0, v_col + hp)),
            pl.BlockSpec(bias_block, lambda hp, bi, rb: (layer * pairs + hp, 0, 0, 0)),
        ],
        out_specs=pl.BlockSpec((1, tq, lanes), lambda hp, bi, rb: (bi, rb, hp)),
        out_shape=jax.ShapeDtypeStruct((b, t, NA_WIDTH), BF16),
        compiler_params=_params(("parallel", "parallel", "arbitrary"), vmem),
        name="na_attn",
    )(h3, h3, h3, bias)


def _conv_mix_kernel(a_ref, g_ref, ap_ref, gp_ref, an_ref, gn_ref, cw_ref, cb_ref, lg_ref, lb_ref,
                     yna_ref, ymem_ref, gna_ref, gconv_ref, gmem_ref, wpa_ref, wpb_ref, wpc_ref,
                     o_ref, u_ref, c_ref, yconv_ref, part_ref, *, tiles_per_seq):
    ti = pl.program_id(0) % tiles_per_seq
    tm = a_ref.shape[0]
    d = o_ref.shape[1]
    pad = CONV_KERNEL // 2
    rc = CONV_ROW_CHUNK

    def glu(a, g):
        return a.astype(F32) * _sigmoid(g.astype(F32))

    u_ref[CONV_HALO:CONV_HALO + tm, :] = glu(a_ref[...], g_ref[...])
    u_ref[0:CONV_HALO, :] = jnp.where(ti > 0, glu(ap_ref[...], gp_ref[...]), 0.0)
    u_ref[CONV_HALO + tm:2 * CONV_HALO + tm, :] = jnp.where(ti < tiles_per_seq - 1, glu(an_ref[...], gn_ref[...]), 0.0)

    col_chunks = [slice(n * MIX_CHUNK, (n + 1) * MIX_CHUNK) for n in range(d // MIX_CHUNK)]

    def side_projection(sl):
        ya = jnp.dot(yna_ref[...], wpa_ref[:, sl], preferred_element_type=F32)
        yc = jnp.dot(ymem_ref[...], wpc_ref[:, sl], preferred_element_type=F32)
        part_ref[:, sl] = (_sigmoid(gna_ref[:, sl].astype(F32)) * ya
                           + _sigmoid(gmem_ref[:, sl].astype(F32)) * yc)

    def conv_projection(sl):
        yb = jnp.dot(yconv_ref[...], wpb_ref[:, sl], preferred_element_type=F32)
        mixed = part_ref[:, sl] + _sigmoid(gconv_ref[:, sl].astype(F32)) * yb
        o_ref[:, sl] = mixed.astype(o_ref.dtype)

    def conv_taps(base, ls):
        acc = None
        for r in range(V7X_SUBLANES):
            part = None
            for a in range(-2, 2):
                j = V7X_SUBLANES * a + r + pad
                if not 0 <= j < CONV_KERNEL:
                    continue
                row0 = base + CONV_HALO + V7X_SUBLANES * a
                term = u_ref[row0:row0 + rc + V7X_SUBLANES, ls] * cw_ref[j:j + 1, ls]
                part = term if part is None else part + term
            part = part[r:r + rc]
            acc = part if acc is None else acc + part
        c_ref[base:base + rc, ls] = acc

    def conv_norm(base):
        y = _layer_norm(c_ref[base:base + rc, :] + cb_ref[...], lg_ref[...], lb_ref[...])
        y = y * _sigmoid(y)
        yconv_ref[base:base + rc, :] = y.astype(yconv_ref.dtype)

    for sl in col_chunks:
        side_projection(sl)
    for base in range(0, tm, rc):
        for lc in range(CONV_WIDTH // V7X_LANES):
            conv_taps(base, slice(lc * V7X_LANES, (lc + 1) * V7X_LANES))
        conv_norm(base)
    for sl in col_chunks:
        conv_projection(sl)


def _conv_mix(h, y_na, y_mem, conv_w, conv_b, ln_g, ln_b, w_pa, w_pb, w_pc, layer, *, t, a_col, g_col, gate_col):
    m = h.shape[0]
    d = w_pa.shape[2]
    tm = min(MIX_TM, t)
    assert t % tm == 0 and tm % CONV_HALO == 0 and tm % CONV_ROW_CHUNK == 0 and d % MIX_CHUNK == 0
    hpt = tm // CONV_HALO
    last_halo = m // CONV_HALO - 1
    cw = CONV_WIDTH

    def rows(width, col=0):
        return pl.BlockSpec((tm, width), lambda i: (i, col))

    def prev(col):
        return pl.BlockSpec((CONV_HALO, cw), lambda i: (jnp.maximum(i * hpt - 1, 0), col))

    def nxt(col):
        return pl.BlockSpec((CONV_HALO, cw), lambda i: (jnp.minimum((i + 1) * hpt, last_halo), col))

    def vec(nrows):
        return pl.BlockSpec((None, nrows, cw), lambda i: (layer, 0, 0))

    def whole(w):
        return pl.BlockSpec((None,) + w.shape[1:], lambda i: (layer, 0, 0), pipeline_mode=pl.Buffered(1))

    vmem = 2 * tm * (2 * cw + NA_WIDTH + MEM_WIDTH + 3 * d + d) * 2 + (2 * NA_WIDTH + MEM_WIDTH) * d * 2 \
        + (tm + 2 * CONV_HALO) * cw * 4 + tm * cw * (4 + 2) + tm * d * 4
    return pl.pallas_call(
        functools.partial(_conv_mix_kernel, tiles_per_seq=t // tm),
        grid=(m // tm,),
        in_specs=[rows(cw, a_col), rows(cw, g_col), prev(a_col), prev(g_col), nxt(a_col), nxt(g_col),
                  vec(CONV_KERNEL), vec(1), vec(1), vec(1),
                  rows(NA_WIDTH), rows(MEM_WIDTH),
                  rows(d, gate_col), rows(d, gate_col + 1), rows(d, gate_col + 2),
                  whole(w_pa), whole(w_pb), whole(w_pc)],
        out_specs=rows(d),
        out_shape=jax.ShapeDtypeStruct((m, d), BF16),
        scratch_shapes=[pltpu.VMEM((tm + 2 * CONV_HALO, cw), F32), pltpu.VMEM((tm, cw), F32),
                        pltpu.VMEM((tm, cw), BF16), pltpu.VMEM((tm, d), F32)],
        compiler_params=_params(("parallel",), vmem),
        name="conv_mix",
    )(h, h, h, h, h, h, conv_w, conv_b, ln_g, ln_b, y_na, y_mem, h, h, h, w_pa, w_pb, w_pc)


def _mem_attn_kernel(*refs):
    q_refs, kv_ref, o_ref = refs[:MEM_HEADS], refs[MEM_HEADS], refs[MEM_HEADS + 1]
    hd = MEM_HEAD_DIM

    def scores(h):
        k = kv_ref[0, :, h * hd:(h + 1) * hd]
        s = lax.dot_general(q_refs[h][0], k, (((1,), (1,)), ((), ())), preferred_element_type=F32)
        return s * (hd ** -0.5)

    pending = [scores(h) for h in range(MEM_HEADS)]
    for h, s in enumerate(pending):
        v = kv_ref[0, :, MEM_WIDTH + h * hd:MEM_WIDTH + (h + 1) * hd]
        m = jnp.max(s, axis=-1, keepdims=True)
        p = jnp.exp(s - m)
        l = jnp.sum(p, axis=-1, keepdims=True)
        o = jnp.dot(p.astype(BF16), v, preferred_element_type=F32) / l
        o_ref[0, :, h * hd:(h + 1) * hd] = o.astype(o_ref.dtype)


def _mem_attention(h3, kv3, *, q_col):
    b, t, _ = h3.shape
    mt = kv3.shape[1]
    tq = min(MEM_TQ, t)
    assert t % tq == 0
    hd = MEM_HEAD_DIM
    vmem = 2 * (2 * tq * MEM_WIDTH * 2 + mt * 2 * MEM_WIDTH * 2) + 2 * MEM_HEADS * tq * mt * 4

    def q_spec(head):
        return pl.BlockSpec((1, tq, hd), lambda bi, ti: (bi, ti, q_col + head))

    return pl.pallas_call(
        _mem_attn_kernel,
        grid=(b, t // tq),
        in_specs=[q_spec(head) for head in range(MEM_HEADS)]
        + [pl.BlockSpec((1, mt, 2 * MEM_WIDTH), lambda bi, ti: (bi, 0, 0))],
        out_specs=pl.BlockSpec((1, tq, MEM_WIDTH), lambda bi, ti: (bi, ti, 0)),
        out_shape=jax.ShapeDtypeStruct((b, t, MEM_WIDTH), BF16),
        compiler_params=_params(("parallel", "parallel"), vmem),
        name="mem_attn",
    )(*([h3] * MEM_HEADS), kv3)


def _proj_ln_kernel(mixed_ref, x_ref, wo_ref, g_ref, b_ref, o_ref, *, alpha):
    tm = x_ref.shape[0]
    for k in range(PROJ_ROW_GROUPS):
        rs = slice(k * tm // PROJ_ROW_GROUPS, (k + 1) * tm // PROJ_ROW_GROUPS)
        z = alpha * x_ref[rs, :] + jnp.dot(mixed_ref[rs, :], wo_ref[...], preferred_element_type=F32)
        o_ref[rs, :] = _layer_norm(z, g_ref[...], b_ref[...])


def _proj_ln(mixed, x, w_o, ln_g, ln_b, layer, *, alpha):
    m, d = x.shape
    tm = min(PROJ_TM, m)
    assert m % tm == 0
    vmem = 2 * tm * d * (2 + 4 + 4) + d * d * 2 + 2 * tm * d * 4
    return pl.pallas_call(
        functools.partial(_proj_ln_kernel, alpha=alpha),
        grid=(m // tm,),
        in_specs=[pl.BlockSpec((tm, d), lambda i: (i, 0)),
                  pl.BlockSpec((tm, d), lambda i: (i, 0)),
                  pl.BlockSpec((None, d, d), lambda i: (layer, 0, 0), pipeline_mode=pl.Buffered(1)),
                  pl.BlockSpec((None, 1, d), lambda i: (layer, 0, 0)),
                  pl.BlockSpec((None, 1, d), lambda i: (layer, 0, 0))],
        out_specs=pl.BlockSpec((tm, d), lambda i: (i, 0)),
        out_shape=jax.ShapeDtypeStruct((m, d), F32),
        compiler_params=_params(("parallel",), vmem),
        name="proj_ln",
    )(mixed, x, w_o, ln_g, ln_b)


def _ffn_kernel(x_hbm, wg_ref, wu_ref, wout_ref, g_ref, b_ref, o_ref, x_buf, xb_ref, x_sem, *, alpha):
    i = pl.program_id(0)
    j = pl.program_id(1)
    tm = x_buf.shape[0]

    def x_copy(tile):
        return pltpu.make_async_copy(x_hbm.at[pl.ds(pl.multiple_of(tile * tm, tm), tm), :], x_buf, x_sem)

    @pl.when((i == 0) & (j == 0))
    def _first_fetch():
        x_copy(0).start()

    @pl.when(j == 0)
    def _init():
        x_copy(i).wait()
        x = x_buf[...]
        xb_ref[...] = x.astype(BF16)
        o_ref[...] = alpha * x

    @pl.when((j == 1) & (i + 1 < pl.num_programs(0)))
    def _prefetch():
        x_copy(i + 1).start()

    xb = xb_ref[...]
    gate = jnp.dot(xb, wg_ref[...], preferred_element_type=F32)
    up = jnp.dot(xb, wu_ref[...], preferred_element_type=F32)
    act = (gate * _sigmoid(gate)) * up
    o_ref[...] += jnp.dot(act.astype(BF16), wout_ref[...], preferred_element_type=F32)

    @pl.when(j == pl.num_programs(1) - 1)
    def _finish():
        o_ref[...] = _layer_norm(o_ref[...], g_ref[...], b_ref[...])


def _ffn(x, w_in, w_out, ln_g, ln_b, layer, *, alpha):
    m, d = x.shape
    f = w_out.shape[1]
    tm = min(FFN_TM, m)
    tf = FFN_TF
    assert m % tm == 0 and f % tf == 0
    nf = f // tf
    assert nf >= 2
    vmem = 2 * tm * d * 4 + tm * d * (2 + 4) + 2 * 3 * d * tf * 2 + 3 * tm * tf * 4
    return pl.pallas_call(
        functools.partial(_ffn_kernel, alpha=alpha),
        grid=(m // tm, nf),
        in_specs=[pl.BlockSpec(memory_space=pl.ANY),
                  pl.BlockSpec((None, d, tf), lambda i, j: (layer, 0, j)),
                  pl.BlockSpec((None, d, tf), lambda i, j: (layer, 0, nf + j)),
                  pl.BlockSpec((None, tf, d), lambda i, j: (layer, j, 0)),
                  pl.BlockSpec((None, 1, d), lambda i, j: (layer, 0, 0)),
                  pl.BlockSpec((None, 1, d), lambda i, j: (layer, 0, 0))],
        out_specs=pl.BlockSpec((tm, d), lambda i, j: (i, 0)),
        out_shape=jax.ShapeDtypeStruct((m, d), F32),
        scratch_shapes=[pltpu.VMEM((tm, d), F32), pltpu.VMEM((tm, d), BF16), pltpu.SemaphoreType.DMA(())],
        compiler_params=_params(("arbitrary", "arbitrary"), vmem),
        name="ffn",
    )(x, w_in, w_in, w_out, ln_g, ln_b)


def _regroup_w_in(w_in):
    gates_off = 3 * NA_WIDTH + 2 * CONV_WIDTH + MEM_WIDTH
    pad = -w_in.shape[2] % IN_PROJ_TN
    zeros = jnp.zeros(w_in.shape[:2] + (pad,), BF16)
    return jnp.concatenate([w_in[:, :, gates_off:].astype(BF16), w_in[:, :, :gates_off].astype(BF16), zeros], axis=2)


def _prepare_params(w_in, w_mem_kv, rpb, conv_w, conv_b, conv_ln_g, conv_ln_b, w_pa, w_pb, w_pc, w_o,
                    ln1_g, ln1_b, w_ffn_in, w_ffn_out, ln2_g, ln2_b):
    depth, heads = rpb.shape[:2]

    def row(v):
        return v.reshape(depth, 1, v.shape[1])

    return dict(
        w_in=_regroup_w_in(w_in),
        w_mem_kv=w_mem_kv.astype(BF16),
        na_bias=_na_bias_table(rpb.reshape((depth * heads,) + rpb.shape[2:])),
        conv_w=conv_w, conv_b=row(conv_b), conv_ln_g=row(conv_ln_g), conv_ln_b=row(conv_ln_b),
        w_pa=w_pa.astype(BF16), w_pb=w_pb.astype(BF16), w_pc=w_pc.astype(BF16),
        w_o=w_o.astype(BF16), ln1_g=row(ln1_g), ln1_b=row(ln1_b),
        w_ffn_in=w_ffn_in.astype(BF16), w_ffn_out=w_ffn_out.astype(BF16),
        ln2_g=row(ln2_g), ln2_b=row(ln2_b),
    )


def _encoder_layer(x, mem, b, t, p, layer, *, alpha):
    d = x.shape[1]
    gate_col = 0
    qkv_off = 3 * d
    q_col = qkv_off // V7X_LANES
    k_col = (qkv_off + NA_WIDTH) // V7X_LANES
    v_col = (qkv_off + 2 * NA_WIDTH) // V7X_LANES
    u_off = qkv_off + 3 * NA_WIDTH
    assert u_off % CONV_WIDTH == 0
    a_col = u_off // CONV_WIDTH
    qm_off = u_off + 2 * CONV_WIDTH
    assert qm_off % MEM_HEAD_DIM == 0 and qkv_off % V7X_LANES == 0
    qm_col = qm_off // MEM_HEAD_DIM

    h = _matmul(x, p["w_in"], layer, tm=IN_PROJ_TM, tn=IN_PROJ_TN, name="in_proj")
    h3 = h.reshape(b, t, h.shape[1])
    y_na = _na_attention(h3, p["na_bias"], layer, q_col=q_col, k_col=k_col, v_col=v_col)
    mt = mem.shape[1]
    kv = _matmul(mem.reshape(b * mt, d), p["w_mem_kv"], layer, tm=b * mt, tn=MEM_KV_TN, name="mem_kv")
    y_mem = _mem_attention(h3, kv.reshape(b, mt, 2 * MEM_WIDTH), q_col=qm_col)
    mixed = _conv_mix(h, y_na.reshape(b * t, NA_WIDTH), y_mem.reshape(b * t, MEM_WIDTH),
                      p["conv_w"], p["conv_b"], p["conv_ln_g"], p["conv_ln_b"],
                      p["w_pa"], p["w_pb"], p["w_pc"], layer,
                      t=t, a_col=a_col, g_col=a_col + 1, gate_col=gate_col)
    x = _proj_ln(mixed, x, p["w_o"], p["ln1_g"], p["ln1_b"], layer, alpha=alpha)
    return _ffn(x, p["w_ffn_in"], p["w_ffn_out"], p["ln2_g"], p["ln2_b"], layer, alpha=alpha)


def _run_trunk(x, mem, p, depth, alpha):
    b, t, d = x.shape
    y = x.reshape(b * t, d)
    for layer in range(depth):
        y = _encoder_layer(y, mem, b, t, p, layer, alpha=alpha)
    return y.reshape(b, t, d)


def kernel(x_prompt, x_sample, mem_prompt, mem_sample, w_in, w_mem_kv, rpb, conv_w, conv_b, conv_ln_g,
           conv_ln_b, w_pa, w_pb, w_pc, w_o, ln1_g, ln1_b, w_ffn_in, w_ffn_out, ln2_g, ln2_b):
    depth = w_in.shape[0]
    alpha = (2 * depth) ** 0.25
    p = _prepare_params(w_in, w_mem_kv, rpb, conv_w, conv_b, conv_ln_g, conv_ln_b, w_pa, w_pb, w_pc, w_o,
                        ln1_g, ln1_b, w_ffn_in, w_ffn_out, ln2_g, ln2_b)
    y_prompt = _run_trunk(x_prompt, mem_prompt, p, depth, alpha)
    y_sample = _run_trunk(x_sample, mem_sample, p, depth, alpha)
    return (y_prompt, y_sample)
```

```python
import functools

import jax
import jax.numpy as jnp
import numpy as np
from jax import lax
from jax.experimental import pallas as pl
from jax.experimental.pallas import tpu as pltpu

F32 = jnp.float32
BF16 = jnp.bfloat16

GRID_W = 64
NA_HEADS = 12
NA_HEAD_DIM = 64
NA_WIDTH = NA_HEADS * NA_HEAD_DIM
NA_WIN_ROWS = 8
NA_WIN_COLS = 16
RPB_ROWS = 2 * NA_WIN_ROWS - 1
RPB_COLS = 2 * NA_WIN_COLS - 1
CONV_WIDTH = 768
CONV_KERNEL = 31
MEM_HEADS = 4
MEM_HEAD_DIM = 128
MEM_WIDTH = MEM_HEADS * MEM_HEAD_DIM
LN_EPS = 1e-5
MASK_VALUE = -1e30
NEG_LOG2_E = -1.4426950408889634

V7X_LANES = 128
V7X_SUBLANES = 8
V7X_VMEM_BYTES = 64 * 1024 * 1024
V7X_VMEM_RESERVE_BYTES = 8 * 1024 * 1024
KERNEL_TEMP_BYTES = 16 * 1024 * 1024

IN_PROJ_TM = 1024
IN_PROJ_TN = 1536
MEM_KV_TN = 512
NA_ROWS_PER_STEP = 32
NA_SCORE_LOOKAHEAD = 3
CONV_HALO = 16
CONV_ROW_CHUNK = 128
MEM_TQ = 1024
MIX_TM = 512
MIX_CHUNK = 512
PROJ_TM = 512
PROJ_ROW_GROUPS = 4
FFN_TM = 1024
FFN_TF = 512


def _params(semantics, block_bytes):
    limit = min(block_bytes + KERNEL_TEMP_BYTES, V7X_VMEM_BYTES - V7X_VMEM_RESERVE_BYTES)
    return pltpu.CompilerParams(dimension_semantics=semantics, vmem_limit_bytes=int(limit))


def _sigmoid(x):
    return 1.0 / (1.0 + jnp.exp2(x * NEG_LOG2_E))


def _layer_norm(z, g, b):
    mu = jnp.mean(z, axis=-1, keepdims=True)
    zc = z - mu
    var = jnp.mean(zc * zc, axis=-1, keepdims=True)
    return zc * lax.rsqrt(var + LN_EPS) * g + b


def _matmul_kernel(x_ref, w_ref, o_ref, xb_ref):
    @pl.when(pl.program_id(1) == 0)
    def _cast():
        xb_ref[...] = x_ref[...].astype(BF16)

    o_ref[...] = jnp.dot(xb_ref[...], w_ref[...], preferred_element_type=F32).astype(o_ref.dtype)


def _tile_columns(w, tn):
    layers, k, n = w.shape
    return w.reshape(layers, k, n // tn, tn).transpose(0, 2, 1, 3)


def _matmul(x, w, layer, *, tm, name):
    m, k = x.shape
    n_tiles, tn = w.shape[1], w.shape[3]
    n = n_tiles * tn
    tm = min(tm, m)
    assert m % tm == 0
    vmem = 2 * tm * k * 4 + tm * k * 2 + 2 * k * tn * 2 + 2 * tm * tn * 2
    return pl.pallas_call(
        _matmul_kernel,
        grid=(m // tm, n_tiles),
        in_specs=[pl.BlockSpec((tm, k), lambda i, j: (i, 0)),
                  pl.BlockSpec((None, None, k, tn), lambda i, j: (layer, j, 0, 0))],
        out_specs=pl.BlockSpec((tm, tn), lambda i, j: (i, j)),
        out_shape=jax.ShapeDtypeStruct((m, n), BF16),
        scratch_shapes=[pltpu.VMEM((tm, k), BF16)],
        compiler_params=_params(("parallel", "arbitrary"), vmem),
        name=name,
    )(x, w)


def _na_bias_table(rpb):
    q = np.arange(GRID_W)[:, None]
    c = np.arange(GRID_W)[None, :]
    col_start = np.clip(q - NA_WIN_COLS // 2, 0, GRID_W - NA_WIN_COLS)
    valid = (c >= col_start) & (c < col_start + NA_WIN_COLS)
    select = (np.arange(RPB_COLS)[:, None, None] == (c - q + NA_WIN_COLS - 1)[None]) & valid[None]
    t = jnp.einsum("hrk,kqc->hrqc", rpb.astype(F32), jnp.asarray(select, F32),
                   precision=lax.Precision.HIGHEST)
    t = jnp.where(valid[None, None], t, MASK_VALUE)
    return jnp.concatenate([t[:, :-1], t[:, 1:]], axis=-1)


def _na_kernel(q_ref, k_ref, v_ref, bias_ref, o_ref, *, n_rows):
    rb = pl.program_id(2)
    win = NA_WIN_ROWS * GRID_W
    lane = lax.broadcasted_iota(jnp.int32, (GRID_W, 2 * NA_HEAD_DIM), 1)
    first_head = lane < NA_HEAD_DIM
    scale = NA_HEAD_DIM ** -0.5

    def window_start(i):
        r = rb * NA_ROWS_PER_STEP + i
        start = jnp.clip(r - NA_WIN_ROWS // 2, 0, n_rows - NA_WIN_ROWS)
        return pl.multiple_of(start * GRID_W, GRID_W), r - start

    def scores(i):
        tok, shift = window_start(i)
        q = q_ref[0, i * GRID_W:(i + 1) * GRID_W, :] * scale
        kw = k_ref[0, pl.ds(tok, win), :]
        zero = jnp.zeros_like(q)
        q2 = jnp.concatenate([jnp.where(first_head, q, zero), jnp.where(first_head, zero, q)], axis=0)
        s = lax.dot_general(q2, kw, (((1,), (1,)), ((), ())), preferred_element_type=F32)
        first_row = NA_WIN_ROWS - 1 - shift
        bias = jnp.concatenate(
            [jnp.concatenate([bias_ref[hh, first_row + 2 * g] for g in range(NA_WIN_ROWS // 2)], axis=1)
             for hh in range(2)], axis=0)
        return s + bias

    pending = [scores(i) for i in range(NA_SCORE_LOOKAHEAD)]
    for i in range(NA_ROWS_PER_STEP):
        s = pending.pop(0)
        if i + NA_SCORE_LOOKAHEAD < NA_ROWS_PER_STEP:
            pending.append(scores(i + NA_SCORE_LOOKAHEAD))
        tok, _ = window_start(i)
        vw = v_ref[0, pl.ds(tok, win), :]
        m = jnp.max(s, axis=-1, keepdims=True)
        p = jnp.exp(s - m)
        l = jnp.sum(p, axis=-1, keepdims=True)
        o2 = jnp.dot(p.astype(BF16), vw, preferred_element_type=F32) / l
        o = jnp.where(first_head, o2[:GRID_W], o2[GRID_W:])
        o_ref[0, i * GRID_W:(i + 1) * GRID_W, :] = o.astype(o_ref.dtype)


def _na_attention(h3, bias, layer, *, q_col, k_col, v_col):
    b, t, _ = h3.shape
    n_rows = t // GRID_W
    pairs = NA_HEADS // 2
    tq = NA_ROWS_PER_STEP * GRID_W
    assert n_rows % NA_ROWS_PER_STEP == 0 and n_rows >= NA_WIN_ROWS
    lanes = 2 * NA_HEAD_DIM
    bias_block = (2,) + bias.shape[1:]
    bias_bytes = 2 * bias.shape[1] * bias.shape[2] * bias.shape[3] * 4
    vmem = 2 * (2 * tq * lanes * 2 + 2 * t * lanes * 2 + bias_bytes)
    return pl.pallas_call(
        functools.partial(_na_kernel, n_rows=n_rows),
        grid=(pairs, b, n_rows // NA_ROWS_PER_STEP),
        in_specs=[
            pl.BlockSpec((1, tq, lanes), lambda hp, bi, rb: (bi, rb, q_col + hp)),
            pl.BlockSpec((1, t, lanes), lambda hp, bi, rb: (bi, 0, k_col + hp)),
            pl.BlockSpec((1, t, lanes), lambda hp, bi, rb: (bi, 0, v_col + hp)),
            pl.BlockSpec(bias_block, lambda hp, bi, rb: (layer * pairs + hp, 0, 0, 0)),
        ],
        out_specs=pl.BlockSpec((1, tq, lanes), lambda hp, bi, rb: (bi, rb, hp)),
        out_shape=jax.ShapeDtypeStruct((b, t, NA_WIDTH), BF16),
        compiler_params=_params(("parallel", "parallel", "arbitrary"), vmem),
        name="na_attn",
    )(h3, h3, h3, bias)


def _conv_mix_kernel(a_ref, g_ref, ap_ref, gp_ref, an_ref, gn_ref, cw_ref, cb_ref, lg_ref, lb_ref,
                     yna_ref, ymem_ref, gna_ref, gconv_ref, gmem_ref, wpa_ref, wpb_ref, wpc_ref,
                     o_ref, u_ref, c_ref, yconv_ref, part_ref, *, tiles_per_seq):
    ti = pl.program_id(0) % tiles_per_seq
    tm = a_ref.shape[0]
    d = o_ref.shape[1]
    pad = CONV_KERNEL // 2
    rc = CONV_ROW_CHUNK

    def glu(a, g):
        return a.astype(F32) * _sigmoid(g.astype(F32))

    u_ref[CONV_HALO:CONV_HALO + tm, :] = glu(a_ref[...], g_ref[...])
    u_ref[0:CONV_HALO, :] = jnp.where(ti > 0, glu(ap_ref[...], gp_ref[...]), 0.0)
    u_ref[CONV_HALO + tm:2 * CONV_HALO + tm, :] = jnp.where(ti < tiles_per_seq - 1, glu(an_ref[...], gn_ref[...]), 0.0)

    col_chunks = [slice(n * MIX_CHUNK, (n + 1) * MIX_CHUNK) for n in range(d // MIX_CHUNK)]

    def side_projection(sl):
        ya = jnp.dot(yna_ref[...], wpa_ref[:, sl], preferred_element_type=F32)
        yc = jnp.dot(ymem_ref[...], wpc_ref[:, sl], preferred_element_type=F32)
        part_ref[:, sl] = (_sigmoid(gna_ref[:, sl].astype(F32)) * ya
                           + _sigmoid(gmem_ref[:, sl].astype(F32)) * yc)

    def conv_projection(sl):
        yb = jnp.dot(yconv_ref[...], wpb_ref[:, sl], preferred_element_type=F32)
        mixed = part_ref[:, sl] + _sigmoid(gconv_ref[:, sl].astype(F32)) * yb
        o_ref[:, sl] = mixed.astype(o_ref.dtype)

    def conv_taps(base, ls):
        acc = None
        for r in range(V7X_SUBLANES):
            part = None
            for a in range(-2, 2):
                j = V7X_SUBLANES * a + r + pad
                if not 0 <= j < CONV_KERNEL:
                    continue
                row0 = base + CONV_HALO + V7X_SUBLANES * a
                term = u_ref[row0:row0 + rc + V7X_SUBLANES, ls] * cw_ref[j:j + 1, ls]
                part = term if part is None else part + term
            part = part[r:r + rc]
            acc = part if acc is None else acc + part
        c_ref[base:base + rc, ls] = acc

    def conv_norm(base):
        y = _layer_norm(c_ref[base:base + rc, :] + cb_ref[...], lg_ref[...], lb_ref[...])
        y = y * _sigmoid(y)
        yconv_ref[base:base + rc, :] = y.astype(yconv_ref.dtype)

    for sl in col_chunks:
        side_projection(sl)
    for base in range(0, tm, rc):
        for lc in range(CONV_WIDTH // V7X_LANES):
            conv_taps(base, slice(lc * V7X_LANES, (lc + 1) * V7X_LANES))
        conv_norm(base)
    for sl in col_chunks:
        conv_projection(sl)


def _conv_mix(h, y_na, y_mem, conv_w, conv_b, ln_g, ln_b, w_pa, w_pb, w_pc, layer, *, t, a_col, g_col, gate_col):
    m = h.shape[0]
    d = w_pa.shape[2]
    tm = min(MIX_TM, t)
    assert t % tm == 0 and tm % CONV_HALO == 0 and tm % CONV_ROW_CHUNK == 0 and d % MIX_CHUNK == 0
    hpt = tm // CONV_HALO
    last_halo = m // CONV_HALO - 1
    cw = CONV_WIDTH

    def rows(width, col=0):
        return pl.BlockSpec((tm, width), lambda i: (i, col))

    def prev(col):
        return pl.BlockSpec((CONV_HALO, cw), lambda i: (jnp.maximum(i * hpt - 1, 0), col))

    def nxt(col):
        return pl.BlockSpec((CONV_HALO, cw), lambda i: (jnp.minimum((i + 1) * hpt, last_halo), col))

    def vec(nrows):
        return pl.BlockSpec((None, nrows, cw), lambda i: (layer, 0, 0))

    def whole(w):
        return pl.BlockSpec((None,) + w.shape[1:], lambda i: (layer, 0, 0), pipeline_mode=pl.Buffered(1))

    vmem = 2 * tm * (2 * cw + NA_WIDTH + MEM_WIDTH + 3 * d + d) * 2 + (2 * NA_WIDTH + MEM_WIDTH) * d * 2 \
        + (tm + 2 * CONV_HALO) * cw * 4 + tm * cw * (4 + 2) + tm * d * 4
    return pl.pallas_call(
        functools.partial(_conv_mix_kernel, tiles_per_seq=t // tm),
        grid=(m // tm,),
        in_specs=[rows(cw, a_col), rows(cw, g_col), prev(a_col), prev(g_col), nxt(a_col), nxt(g_col),
                  vec(CONV_KERNEL), vec(1), vec(1), vec(1),
                  rows(NA_WIDTH), rows(MEM_WIDTH),
                  rows(d, gate_col), rows(d, gate_col + 1), rows(d, gate_col + 2),
                  whole(w_pa), whole(w_pb), whole(w_pc)],
        out_specs=rows(d),
        out_shape=jax.ShapeDtypeStruct((m, d), BF16),
        scratch_shapes=[pltpu.VMEM((tm + 2 * CONV_HALO, cw), F32), pltpu.VMEM((tm, cw), F32),
                        pltpu.VMEM((tm, cw), BF16), pltpu.VMEM((tm, d), F32)],
        compiler_params=_params(("parallel",), vmem),
        name="conv_mix",
    )(h, h, h, h, h, h, conv_w, conv_b, ln_g, ln_b, y_na, y_mem, h, h, h, w_pa, w_pb, w_pc)


def _mem_attn_kernel(*refs):
    q_refs, kv_ref, o_ref = refs[:MEM_HEADS], refs[MEM_HEADS], refs[MEM_HEADS + 1]
    hd = MEM_HEAD_DIM

    def scores(h):
        k = kv_ref[0, :, h * hd:(h + 1) * hd]
        s = lax.dot_general(q_refs[h][0], k, (((1,), (1,)), ((), ())), preferred_element_type=F32)
        return s * (hd ** -0.5)

    pending = [scores(h) for h in range(MEM_HEADS)]
    for h, s in enumerate(pending):
        v = kv_ref[0, :, MEM_WIDTH + h * hd:MEM_WIDTH + (h + 1) * hd]
        m = jnp.max(s, axis=-1, keepdims=True)
        p = jnp.exp(s - m)
        l = jnp.sum(p, axis=-1, keepdims=True)
        o = jnp.dot(p.astype(BF16), v, preferred_element_type=F32) / l
        o_ref[0, :, h * hd:(h + 1) * hd] = o.astype(o_ref.dtype)


def _mem_attention(h3, kv3, *, q_col):
    b, t, _ = h3.shape
    mt = kv3.shape[1]
    tq = min(MEM_TQ, t)
    assert t % tq == 0
    hd = MEM_HEAD_DIM
    vmem = 2 * (2 * tq * MEM_WIDTH * 2 + mt * 2 * MEM_WIDTH * 2) + 2 * MEM_HEADS * tq * mt * 4

    def q_spec(head):
        return pl.BlockSpec((1, tq, hd), lambda bi, ti: (bi, ti, q_col + head))

    return pl.pallas_call(
        _mem_attn_kernel,
        grid=(b, t // tq),
        in_specs=[q_spec(head) for head in range(MEM_HEADS)]
        + [pl.BlockSpec((1, mt, 2 * MEM_WIDTH), lambda bi, ti: (bi, 0, 0))],
        out_specs=pl.BlockSpec((1, tq, MEM_WIDTH), lambda bi, ti: (bi, ti, 0)),
        out_shape=jax.ShapeDtypeStruct((b, t, MEM_WIDTH), BF16),
        compiler_params=_params(("parallel", "parallel"), vmem),
        name="mem_attn",
    )(*([h3] * MEM_HEADS), kv3)


def _proj_ln_kernel(mixed_ref, x_ref, wo_ref, g_ref, b_ref, o_ref, *, alpha):
    tm = x_ref.shape[0]
    for k in range(PROJ_ROW_GROUPS):
        rs = slice(k * tm // PROJ_ROW_GROUPS, (k + 1) * tm // PROJ_ROW_GROUPS)
        z = alpha * x_ref[rs, :] + jnp.dot(mixed_ref[rs, :], wo_ref[...], preferred_element_type=F32)
        o_ref[rs, :] = _layer_norm(z, g_ref[...], b_ref[...])


def _proj_ln(mixed, x, w_o, ln_g, ln_b, layer, *, alpha):
    m, d = x.shape
    tm = min(PROJ_TM, m)
    assert m % tm == 0
    vmem = 2 * tm * d * (2 + 4 + 4) + d * d * 2 + 2 * tm * d * 4
    return pl.pallas_call(
        functools.partial(_proj_ln_kernel, alpha=alpha),
        grid=(m // tm,),
        in_specs=[pl.BlockSpec((tm, d), lambda i: (i, 0)),
                  pl.BlockSpec((tm, d), lambda i: (i, 0)),
                  pl.BlockSpec((None, d, d), lambda i: (layer, 0, 0), pipeline_mode=pl.Buffered(1)),
                  pl.BlockSpec((None, 1, d), lambda i: (layer, 0, 0)),
                  pl.BlockSpec((None, 1, d), lambda i: (layer, 0, 0))],
        out_specs=pl.BlockSpec((tm, d), lambda i: (i, 0)),
        out_shape=jax.ShapeDtypeStruct((m, d), F32),
        compiler_params=_params(("parallel",), vmem),
        name="proj_ln",
    )(mixed, x, w_o, ln_g, ln_b)


def _ffn_kernel(x_hbm, wg_ref, wu_ref, wout_ref, g_ref, b_ref, o_ref, x_buf, xb_ref, x_sem, *, alpha):
    i = pl.program_id(0)
    j = pl.program_id(1)
    tm = x_buf.shape[0]

    def x_copy(tile):
        return pltpu.make_async_copy(x_hbm.at[pl.ds(pl.multiple_of(tile * tm, tm), tm), :], x_buf, x_sem)

    @pl.when((i == 0) & (j == 0))
    def _first_fetch():
        x_copy(0).start()

    @pl.when(j == 0)
    def _init():
        x_copy(i).wait()
        x = x_buf[...]
        xb_ref[...] = x.astype(BF16)
        o_ref[...] = alpha * x

    @pl.when((j == 1) & (i + 1 < pl.num_programs(0)))
    def _prefetch():
        x_copy(i + 1).start()

    xb = xb_ref[...]
    gate = jnp.dot(xb, wg_ref[...], preferred_element_type=F32)
    up = jnp.dot(xb, wu_ref[...], preferred_element_type=F32)
    act = (gate * _sigmoid(gate)) * up
    o_ref[...] += jnp.dot(act.astype(BF16), wout_ref[...], preferred_element_type=F32)

    @pl.when(j == pl.num_programs(1) - 1)
    def _finish():
        o_ref[...] = _layer_norm(o_ref[...], g_ref[...], b_ref[...])


def _ffn(x, w_in, w_out, ln_g, ln_b, layer, *, alpha):
    m, d = x.shape
    f = w_out.shape[1]
    tm = min(FFN_TM, m)
    tf = w_in.shape[3]
    assert m % tm == 0 and f % tf == 0
    nf = f // tf
    assert nf >= 2
    vmem = 2 * tm * d * 4 + tm * d * (2 + 4) + 2 * 3 * d * tf * 2 + 3 * tm * tf * 4
    return pl.pallas_call(
        functools.partial(_ffn_kernel, alpha=alpha),
        grid=(m // tm, nf),
        in_specs=[pl.BlockSpec(memory_space=pl.ANY),
                  pl.BlockSpec((None, None, d, tf), lambda i, j: (layer, j, 0, 0)),
                  pl.BlockSpec((None, None, d, tf), lambda i, j: (layer, nf + j, 0, 0)),
                  pl.BlockSpec((None, tf, d), lambda i, j: (layer, j, 0)),
                  pl.BlockSpec((None, 1, d), lambda i, j: (layer, 0, 0)),
                  pl.BlockSpec((None, 1, d), lambda i, j: (layer, 0, 0))],
        out_specs=pl.BlockSpec((tm, d), lambda i, j: (i, 0)),
        out_shape=jax.ShapeDtypeStruct((m, d), F32),
        scratch_shapes=[pltpu.VMEM((tm, d), F32), pltpu.VMEM((tm, d), BF16), pltpu.SemaphoreType.DMA(())],
        compiler_params=_params(("arbitrary", "arbitrary"), vmem),
        name="ffn",
    )(x, w_in, w_in, w_out, ln_g, ln_b)


def _regroup_w_in(w_in):
    gates_off = 3 * NA_WIDTH + 2 * CONV_WIDTH + MEM_WIDTH
    pad = -w_in.shape[2] % IN_PROJ_TN
    zeros = jnp.zeros(w_in.shape[:2] + (pad,), BF16)
    return jnp.concatenate([w_in[:, :, gates_off:].astype(BF16), w_in[:, :, :gates_off].astype(BF16), zeros], axis=2)


def _prepare_params(w_in, w_mem_kv, rpb, conv_w, conv_b, conv_ln_g, conv_ln_b, w_pa, w_pb, w_pc, w_o,
                    ln1_g, ln1_b, w_ffn_in, w_ffn_out, ln2_g, ln2_b):
    depth, heads = rpb.shape[:2]

    def row(v):
        return v.reshape(depth, 1, v.shape[1])

    return dict(
        w_in=_tile_columns(_regroup_w_in(w_in), IN_PROJ_TN),
        w_mem_kv=_tile_columns(w_mem_kv.astype(BF16), MEM_KV_TN),
        na_bias=_na_bias_table(rpb.reshape((depth * heads,) + rpb.shape[2:])),
        conv_w=conv_w, conv_b=row(conv_b), conv_ln_g=row(conv_ln_g), conv_ln_b=row(conv_ln_b),
        w_pa=w_pa.astype(BF16), w_pb=w_pb.astype(BF16), w_pc=w_pc.astype(BF16),
        w_o=w_o.astype(BF16), ln1_g=row(ln1_g), ln1_b=row(ln1_b),
        w_ffn_in=_tile_columns(w_ffn_in.astype(BF16), FFN_TF), w_ffn_out=w_ffn_out.astype(BF16),
        ln2_g=row(ln2_g), ln2_b=row(ln2_b),
    )


def _encoder_layer(x, mem, b, t, p, layer, *, alpha):
    d = x.shape[1]
    gate_col = 0
    qkv_off = 3 * d
    q_col = qkv_off // V7X_LANES
    k_col = (qkv_off + NA_WIDTH) // V7X_LANES
    v_col = (qkv_off + 2 * NA_WIDTH) // V7X_LANES
    u_off = qkv_off + 3 * NA_WIDTH
    assert u_off % CONV_WIDTH == 0
    a_col = u_off // CONV_WIDTH
    qm_off = u_off + 2 * CONV_WIDTH
    assert qm_off % MEM_HEAD_DIM == 0 and qkv_off % V7X_LANES == 0
    qm_col = qm_off // MEM_HEAD_DIM

    h = _matmul(x, p["w_in"], layer, tm=IN_PROJ_TM, name="in_proj")
    h3 = h.reshape(b, t, h.shape[1])
    y_na = _na_attention(h3, p["na_bias"], layer, q_col=q_col, k_col=k_col, v_col=v_col)
    mt = mem.shape[1]
    kv = _matmul(mem.reshape(b * mt, d), p["w_mem_kv"], layer, tm=b * mt, name="mem_kv")
    y_mem = _mem_attention(h3, kv.reshape(b, mt, 2 * MEM_WIDTH), q_col=qm_col)
    mixed = _conv_mix(h, y_na.reshape(b * t, NA_WIDTH), y_mem.reshape(b * t, MEM_WIDTH),
                      p["conv_w"], p["conv_b"], p["conv_ln_g"], p["conv_ln_b"],
                      p["w_pa"], p["w_pb"], p["w_pc"], layer,
                      t=t, a_col=a_col, g_col=a_col + 1, gate_col=gate_col)
    x = _proj_ln(mixed, x, p["w_o"], p["ln1_g"], p["ln1_b"], layer, alpha=alpha)
    return _ffn(x, p["w_ffn_in"], p["w_ffn_out"], p["ln2_g"], p["ln2_b"], layer, alpha=alpha)


def _run_trunk(x, mem, p, depth, alpha):
    b, t, d = x.shape
    y = x.reshape(b * t, d)
    for layer in range(depth):
        y = _encoder_layer(y, mem, b, t, p, layer, alpha=alpha)
    return y.reshape(b, t, d)


def kernel(x_prompt, x_sample, mem_prompt, mem_sample, w_in, w_mem_kv, rpb, conv_w, conv_b, conv_ln_g,
           conv_ln_b, w_pa, w_pb, w_pc, w_o, ln1_g, ln1_b, w_ffn_in, w_ffn_out, ln2_g, ln2_b):
    depth = w_in.shape[0]
    alpha = (2 * depth) ** 0.25
    p = _prepare_params(w_in, w_mem_kv, rpb, conv_w, conv_b, conv_ln_g, conv_ln_b, w_pa, w_pb, w_pc, w_o,
                        ln1_g, ln1_b, w_ffn_in, w_ffn_out, ln2_g, ln2_b)
    y_prompt = _run_trunk(x_prompt, mem_prompt, p, depth, alpha)
    y_sample = _run_trunk(x_sample, mem_sample, p, depth, alpha)
    return (y_prompt, y_sample)
```

```python
import functools

import jax
import jax.numpy as jnp
import numpy as np
from jax import lax
from jax.experimental import pallas as pl
from jax.experimental.pallas import tpu as pltpu

F32 = jnp.float32
BF16 = jnp.bfloat16

GRID_W = 64
NA_HEADS = 12
NA_HEAD_DIM = 64
NA_WIDTH = NA_HEADS * NA_HEAD_DIM
NA_WIN_ROWS = 8
NA_WIN_COLS = 16
RPB_ROWS = 2 * NA_WIN_ROWS - 1
RPB_COLS = 2 * NA_WIN_COLS - 1
CONV_WIDTH = 768
CONV_KERNEL = 31
MEM_HEADS = 4
MEM_HEAD_DIM = 128
MEM_WIDTH = MEM_HEADS * MEM_HEAD_DIM
LN_EPS = 1e-5
MASK_VALUE = -1e30
NEG_LOG2_E = -1.4426950408889634

V7X_LANES = 128
V7X_SUBLANES = 8
V7X_VMEM_BYTES = 64 * 1024 * 1024
V7X_VMEM_RESERVE_BYTES = 8 * 1024 * 1024
KERNEL_TEMP_BYTES = 16 * 1024 * 1024

IN_PROJ_TM = 1024
IN_PROJ_TN = 1536
MEM_KV_TN = 512
NA_ROWS_PER_STEP = 32
NA_SCORE_LOOKAHEAD = 3
CONV_HALO = 16
CONV_ROW_CHUNK = 128
MEM_TQ = 1024
MIX_TM = 512
MIX_CHUNK = 1024
PROJ_TM = 512
PROJ_ROW_GROUPS = 4
FFN_TM = 1024
FFN_TF = 512


def _params(semantics, block_bytes):
    limit = min(block_bytes + KERNEL_TEMP_BYTES, V7X_VMEM_BYTES - V7X_VMEM_RESERVE_BYTES)
    return pltpu.CompilerParams(dimension_semantics=semantics, vmem_limit_bytes=int(limit))


def _sigmoid(x):
    return 1.0 / (1.0 + jnp.exp2(x * NEG_LOG2_E))


def _layer_norm(z, g, b):
    mu = jnp.mean(z, axis=-1, keepdims=True)
    zc = z - mu
    var = jnp.mean(zc * zc, axis=-1, keepdims=True)
    return zc * lax.rsqrt(var + LN_EPS) * g + b


def _matmul_kernel(x_ref, w_ref, o_ref, xb_ref):
    @pl.when(pl.program_id(1) == 0)
    def _cast():
        xb_ref[...] = x_ref[...].astype(BF16)

    o_ref[...] = jnp.dot(xb_ref[...], w_ref[...], preferred_element_type=F32).astype(o_ref.dtype)


def _matmul(x, w, layer, *, tm, tn, name):
    m, k = x.shape
    n = w.shape[2]
    tm = min(tm, m)
    assert m % tm == 0 and n % tn == 0
    vmem = 2 * tm * k * 4 + tm * k * 2 + 2 * k * tn * 2 + 2 * tm * tn * 2
    return pl.pallas_call(
        _matmul_kernel,
        grid=(m // tm, n // tn),
        in_specs=[pl.BlockSpec((tm, k), lambda i, j: (i, 0)),
                  pl.BlockSpec((None, k, tn), lambda i, j: (layer, 0, j))],
        out_specs=pl.BlockSpec((tm, tn), lambda i, j: (i, j)),
        out_shape=jax.ShapeDtypeStruct((m, n), BF16),
        scratch_shapes=[pltpu.VMEM((tm, k), BF16)],
        compiler_params=_params(("parallel", "arbitrary"), vmem),
        name=name,
    )(x, w)


def _na_bias_table(rpb):
    q = np.arange(GRID_W)[:, None]
    c = np.arange(GRID_W)[None, :]
    col_start = np.clip(q - NA_WIN_COLS // 2, 0, GRID_W - NA_WIN_COLS)
    valid = (c >= col_start) & (c < col_start + NA_WIN_COLS)
    select = (np.arange(RPB_COLS)[:, None, None] == (c - q + NA_WIN_COLS - 1)[None]) & valid[None]
    t = jnp.einsum("hrk,kqc->hrqc", rpb.astype(F32), jnp.asarray(select, F32),
                   precision=lax.Precision.HIGHEST)
    t = jnp.where(valid[None, None], t, MASK_VALUE)
    return jnp.concatenate([t[:, :-1], t[:, 1:]], axis=-1)


def _na_kernel(q_ref, k_ref, v_ref, bias_ref, o_ref, *, n_rows):
    rb = pl.program_id(2)
    win = NA_WIN_ROWS * GRID_W
    lane = lax.broadcasted_iota(jnp.int32, (GRID_W, 2 * NA_HEAD_DIM), 1)
    first_head = lane < NA_HEAD_DIM
    scale = NA_HEAD_DIM ** -0.5

    def window_start(i):
        r = rb * NA_ROWS_PER_STEP + i
        start = jnp.clip(r - NA_WIN_ROWS // 2, 0, n_rows - NA_WIN_ROWS)
        return pl.multiple_of(start * GRID_W, GRID_W), r - start

    def scores(i):
        tok, shift = window_start(i)
        q = q_ref[0, i * GRID_W:(i + 1) * GRID_W, :] * scale
        kw = k_ref[0, pl.ds(tok, win), :]
        zero = jnp.zeros_like(q)
        q2 = jnp.concatenate([jnp.where(first_head, q, zero), jnp.where(first_head, zero, q)], axis=0)
        s = lax.dot_general(q2, kw, (((1,), (1,)), ((), ())), preferred_element_type=F32)
        first_row = NA_WIN_ROWS - 1 - shift
        bias = jnp.concatenate(
            [jnp.concatenate([bias_ref[hh, first_row + 2 * g] for g in range(NA_WIN_ROWS // 2)], axis=1)
             for hh in range(2)], axis=0)
        return s + bias

    pending = [scores(i) for i in range(NA_SCORE_LOOKAHEAD)]
    for i in range(NA_ROWS_PER_STEP):
        s = pending.pop(0)
        if i + NA_SCORE_LOOKAHEAD < NA_ROWS_PER_STEP:
            pending.append(scores(i + NA_SCORE_LOOKAHEAD))
        tok, _ = window_start(i)
        vw = v_ref[0, pl.ds(tok, win), :]
        m = jnp.max(s, axis=-1, keepdims=True)
        p = jnp.exp(s - m)
        l = jnp.sum(p, axis=-1, keepdims=True)
        o2 = jnp.dot(p.astype(BF16), vw, preferred_element_type=F32) / l
        o = jnp.where(first_head, o2[:GRID_W], o2[GRID_W:])
        o_ref[0, i * GRID_W:(i + 1) * GRID_W, :] = o.astype(o_ref.dtype)


def _na_attention(h3, bias, layer, *, q_col, k_col, v_col):
    b, t, _ = h3.shape
    n_rows = t // GRID_W
    pairs = NA_HEADS // 2
    tq = NA_ROWS_PER_STEP * GRID_W
    assert n_rows % NA_ROWS_PER_STEP == 0 and n_rows >= NA_WIN_ROWS
    lanes = 2 * NA_HEAD_DIM
    bias_block = (2,) + bias.shape[1:]
    bias_bytes = 2 * bias.shape[1] * bias.shape[2] * bias.shape[3] * 4
    vmem = 2 * (2 * tq * lanes * 2 + 2 * t * lanes * 2 + bias_bytes)
    return pl.pallas_call(
        functools.partial(_na_kernel, n_rows=n_rows),
        grid=(pairs, b, n_rows // NA_ROWS_PER_STEP),
        in_specs=[
            pl.BlockSpec((1, tq, lanes), lambda hp, bi, rb: (bi, rb, q_col + hp)),
            pl.BlockSpec((1, t, lanes), lambda hp, bi, rb: (bi, 0, k_col + hp)),
            pl.BlockSpec((1, t, lanes), lambda hp, bi, rb: (bi, 0, v_col + hp)),
            pl.BlockSpec(bias_block, lambda hp, bi, rb: (layer * pairs + hp, 0, 0, 0)),
        ],
        out_specs=pl.BlockSpec((1, tq, lanes), lambda hp, bi, rb: (bi, rb, hp)),
        out_shape=jax.ShapeDtypeStruct((b, t, NA_WIDTH), BF16),
        compiler_params=_params(("parallel", "parallel", "arbitrary"), vmem),
        name="na_attn",
    )(h3, h3, h3, bias)


def _conv_mix_kernel(a_ref, g_ref, ap_ref, gp_ref, an_ref, gn_ref, cw_ref, cb_ref, lg_ref, lb_ref,
                     yna_ref, ymem_ref, gna_ref, gconv_ref, gmem_ref, wpa_ref, wpb_ref, wpc_ref,
                     o_ref, u_ref, c_ref, yconv_ref, part_ref, *, tiles_per_seq):
    ti = pl.program_id(0) % tiles_per_seq
    tm = a_ref.shape[0]
    d = o_ref.shape[1]
    pad = CONV_KERNEL // 2
    rc = CONV_ROW_CHUNK

    def glu(a, g):
        return a.astype(F32) * _sigmoid(g.astype(F32))

    u_ref[CONV_HALO:CONV_HALO + tm, :] = glu(a_ref[...], g_ref[...])
    u_ref[0:CONV_HALO, :] = jnp.where(ti > 0, glu(ap_ref[...], gp_ref[...]), 0.0)
    u_ref[CONV_HALO + tm:2 * CONV_HALO + tm, :] = jnp.where(ti < tiles_per_seq - 1, glu(an_ref[...], gn_ref[...]), 0.0)

    col_chunks = [slice(n * MIX_CHUNK, (n + 1) * MIX_CHUNK) for n in range(d // MIX_CHUNK)]

    def side_projection(sl):
        ya = jnp.dot(yna_ref[...], wpa_ref[:, sl], preferred_element_type=F32)
        yc = jnp.dot(ymem_ref[...], wpc_ref[:, sl], preferred_element_type=F32)
        part_ref[:, sl] = (_sigmoid(gna_ref[:, sl].astype(F32)) * ya
                           + _sigmoid(gmem_ref[:, sl].astype(F32)) * yc)

    def conv_projection(sl):
        yb = jnp.dot(yconv_ref[...], wpb_ref[:, sl], preferred_element_type=F32)
        mixed = part_ref[:, sl] + _sigmoid(gconv_ref[:, sl].astype(F32)) * yb
        o_ref[:, sl] = mixed.astype(o_ref.dtype)

    def conv_taps(base, ls):
        acc = None
        for r in range(V7X_SUBLANES):
            part = None
            for a in range(-2, 2):
                j = V7X_SUBLANES * a + r + pad
                if not 0 <= j < CONV_KERNEL:
                    continue
                row0 = base + CONV_HALO + V7X_SUBLANES * a
                term = u_ref[row0:row0 + rc + V7X_SUBLANES, ls] * cw_ref[j:j + 1, ls]
                part = term if part is None else part + term
            part = part[r:r + rc]
            acc = part if acc is None else acc + part
        c_ref[base:base + rc, ls] = acc

    def conv_norm(base):
        y = _layer_norm(c_ref[base:base + rc, :] + cb_ref[...], lg_ref[...], lb_ref[...])
        y = y * _sigmoid(y)
        yconv_ref[base:base + rc, :] = y.astype(yconv_ref.dtype)

    for sl in col_chunks:
        side_projection(sl)
    for base in range(0, tm, rc):
        for lc in range(CONV_WIDTH // V7X_LANES):
            conv_taps(base, slice(lc * V7X_LANES, (lc + 1) * V7X_LANES))
        conv_norm(base)
    for sl in col_chunks:
        conv_projection(sl)


def _conv_mix(h, y_na, y_mem, conv_w, conv_b, ln_g, ln_b, w_pa, w_pb, w_pc, layer, *, t, a_col, g_col, gate_col):
    m = h.shape[0]
    d = w_pa.shape[2]
    tm = min(MIX_TM, t)
    assert t % tm == 0 and tm % CONV_HALO == 0 and tm % CONV_ROW_CHUNK == 0 and d % MIX_CHUNK == 0
    hpt = tm // CONV_HALO
    last_halo = m // CONV_HALO - 1
    cw = CONV_WIDTH

    def rows(width, col=0):
        return pl.BlockSpec((tm, width), lambda i: (i, col))

    def prev(col):
        return pl.BlockSpec((CONV_HALO, cw), lambda i: (jnp.maximum(i * hpt - 1, 0), col))

    def nxt(col):
        return pl.BlockSpec((CONV_HALO, cw), lambda i: (jnp.minimum((i + 1) * hpt, last_halo), col))

    def vec(nrows):
        return pl.BlockSpec((None, nrows, cw), lambda i: (layer, 0, 0))

    def whole(w):
        return pl.BlockSpec((None,) + w.shape[1:], lambda i: (layer, 0, 0), pipeline_mode=pl.Buffered(1))

    vmem = 2 * tm * (2 * cw + NA_WIDTH + MEM_WIDTH + 3 * d + d) * 2 + (2 * NA_WIDTH + MEM_WIDTH) * d * 2 \
        + (tm + 2 * CONV_HALO) * cw * 4 + tm * cw * (4 + 2) + tm * d * 4
    return pl.pallas_call(
        functools.partial(_conv_mix_kernel, tiles_per_seq=t // tm),
        grid=(m // tm,),
        in_specs=[rows(cw, a_col), rows(cw, g_col), prev(a_col), prev(g_col), nxt(a_col), nxt(g_col),
                  vec(CONV_KERNEL), vec(1), vec(1), vec(1),
                  rows(NA_WIDTH), rows(MEM_WIDTH),
                  rows(d, gate_col), rows(d, gate_col + 1), rows(d, gate_col + 2),
                  whole(w_pa), whole(w_pb), whole(w_pc)],
        out_specs=rows(d),
        out_shape=jax.ShapeDtypeStruct((m, d), BF16),
        scratch_shapes=[pltpu.VMEM((tm + 2 * CONV_HALO, cw), F32), pltpu.VMEM((tm, cw), F32),
                        pltpu.VMEM((tm, cw), BF16), pltpu.VMEM((tm, d), F32)],
        compiler_params=_params(("parallel",), vmem),
        name="conv_mix",
    )(h, h, h, h, h, h, conv_w, conv_b, ln_g, ln_b, y_na, y_mem, h, h, h, w_pa, w_pb, w_pc)


def _mem_attn_kernel(*refs):
    q_refs, kv_ref, o_ref = refs[:MEM_HEADS], refs[MEM_HEADS], refs[MEM_HEADS + 1]
    hd = MEM_HEAD_DIM

    def scores(h):
        k = kv_ref[0, :, h * hd:(h + 1) * hd]
        s = lax.dot_general(q_refs[h][0], k, (((1,), (1,)), ((), ())), preferred_element_type=F32)
        return s * (hd ** -0.5)

    pending = [scores(h) for h in range(MEM_HEADS)]
    for h, s in enumerate(pending):
        v = kv_ref[0, :, MEM_WIDTH + h * hd:MEM_WIDTH + (h + 1) * hd]
        m = jnp.max(s, axis=-1, keepdims=True)
        p = jnp.exp(s - m)
        l = jnp.sum(p, axis=-1, keepdims=True)
        o = jnp.dot(p.astype(BF16), v, preferred_element_type=F32) / l
        o_ref[0, :, h * hd:(h + 1) * hd] = o.astype(o_ref.dtype)


def _mem_attention(h3, kv3, *, q_col):
    b, t, _ = h3.shape
    mt = kv3.shape[1]
    tq = min(MEM_TQ, t)
    assert t % tq == 0
    hd = MEM_HEAD_DIM
    vmem = 2 * (2 * tq * MEM_WIDTH * 2 + mt * 2 * MEM_WIDTH * 2) + 2 * MEM_HEADS * tq * mt * 4

    def q_spec(head):
        return pl.BlockSpec((1, tq, hd), lambda bi, ti: (bi, ti, q_col + head))

    return pl.pallas_call(
        _mem_attn_kernel,
        grid=(b, t // tq),
        in_specs=[q_spec(head) for head in range(MEM_HEADS)]
        + [pl.BlockSpec((1, mt, 2 * MEM_WIDTH), lambda bi, ti: (bi, 0, 0))],
        out_specs=pl.BlockSpec((1, tq, MEM_WIDTH), lambda bi, ti: (bi, ti, 0)),
        out_shape=jax.ShapeDtypeStruct((b, t, MEM_WIDTH), BF16),
        compiler_params=_params(("parallel", "parallel"), vmem),
        name="mem_attn",
    )(*([h3] * MEM_HEADS), kv3)


def _proj_ln_kernel(mixed_ref, x_ref, wo_ref, g_ref, b_ref, o_ref, *, alpha):
    tm = x_ref.shape[0]
    for k in range(PROJ_ROW_GROUPS):
        rs = slice(k * tm // PROJ_ROW_GROUPS, (k + 1) * tm // PROJ_ROW_GROUPS)
        z = alpha * x_ref[rs, :] + jnp.dot(mixed_ref[rs, :], wo_ref[...], preferred_element_type=F32)
        o_ref[rs, :] = _layer_norm(z, g_ref[...], b_ref[...])


def _proj_ln(mixed, x, w_o, ln_g, ln_b, layer, *, alpha):
    m, d = x.shape
    tm = min(PROJ_TM, m)
    assert m % tm == 0
    vmem = 2 * tm * d * (2 + 4 + 4) + d * d * 2 + 2 * tm * d * 4
    return pl.pallas_call(
        functools.partial(_proj_ln_kernel, alpha=alpha),
        grid=(m // tm,),
        in_specs=[pl.BlockSpec((tm, d), lambda i: (i, 0)),
                  pl.BlockSpec((tm, d), lambda i: (i, 0)),
                  pl.BlockSpec((None, d, d), lambda i: (layer, 0, 0), pipeline_mode=pl.Buffered(1)),
                  pl.BlockSpec((None, 1, d), lambda i: (layer, 0, 0)),
                  pl.BlockSpec((None, 1, d), lambda i: (layer, 0, 0))],
        out_specs=pl.BlockSpec((tm, d), lambda i: (i, 0)),
        out_shape=jax.ShapeDtypeStruct((m, d), F32),
        compiler_params=_params(("parallel",), vmem),
        name="proj_ln",
    )(mixed, x, w_o, ln_g, ln_b)


def _ffn_kernel(x_hbm, wg_ref, wu_ref, wout_ref, g_ref, b_ref, o_ref, x_buf, xb_ref, x_sem, *, alpha):
    i = pl.program_id(0)
    j = pl.program_id(1)
    tm = x_buf.shape[0]

    def x_copy(tile):
        return pltpu.make_async_copy(x_hbm.at[pl.ds(pl.multiple_of(tile * tm, tm), tm), :], x_buf, x_sem)

    @pl.when((i == 0) & (j == 0))
    def _first_fetch():
        x_copy(0).start()

    @pl.when(j == 0)
    def _init():
        x_copy(i).wait()
        x = x_buf[...]
        xb_ref[...] = x.astype(BF16)
        o_ref[...] = alpha * x

    @pl.when((j == 1) & (i + 1 < pl.num_programs(0)))
    def _prefetch():
        x_copy(i + 1).start()

    xb = xb_ref[...]
    gate = jnp.dot(xb, wg_ref[...], preferred_element_type=F32)
    up = jnp.dot(xb, wu_ref[...], preferred_element_type=F32)
    act = (gate * _sigmoid(gate)) * up
    o_ref[...] += jnp.dot(act.astype(BF16), wout_ref[...], preferred_element_type=F32)

    @pl.when(j == pl.num_programs(1) - 1)
    def _finish():
        o_ref[...] = _layer_norm(o_ref[...], g_ref[...], b_ref[...])


def _ffn(x, w_in, w_out, ln_g, ln_b, layer, *, alpha):
    m, d = x.shape
    f = w_out.shape[1]
    tm = min(FFN_TM, m)
    tf = FFN_TF
    assert m % tm == 0 and f % tf == 0
    nf = f // tf
    assert nf >= 2
    vmem = 2 * tm * d * 4 + tm * d * (2 + 4) + 2 * 3 * d * tf * 2 + 3 * tm * tf * 4
    return pl.pallas_call(
        functools.partial(_ffn_kernel, alpha=alpha),
        grid=(m // tm, nf),
        in_specs=[pl.BlockSpec(memory_space=pl.ANY),
                  pl.BlockSpec((None, d, tf), lambda i, j: (layer, 0, j)),
                  pl.BlockSpec((None, d, tf), lambda i, j: (layer, 0, nf + j)),
                  pl.BlockSpec((None, tf, d), lambda i, j: (layer, j, 0)),
                  pl.BlockSpec((None, 1, d), lambda i, j: (layer, 0, 0)),
                  pl.BlockSpec((None, 1, d), lambda i, j: (layer, 0, 0))],
        out_specs=pl.BlockSpec((tm, d), lambda i, j: (i, 0)),
        out_shape=jax.ShapeDtypeStruct((m, d), F32),
        scratch_shapes=[pltpu.VMEM((tm, d), F32), pltpu.VMEM((tm, d), BF16), pltpu.SemaphoreType.DMA(())],
        compiler_params=_params(("arbitrary", "arbitrary"), vmem),
        name="ffn",
    )(x, w_in, w_in, w_out, ln_g, ln_b)


def _regroup_w_in(w_in):
    gates_off = 3 * NA_WIDTH + 2 * CONV_WIDTH + MEM_WIDTH
    pad = -w_in.shape[2] % IN_PROJ_TN
    zeros = jnp.zeros(w_in.shape[:2] + (pad,), BF16)
    return jnp.concatenate([w_in[:, :, gates_off:].astype(BF16), w_in[:, :, :gates_off].astype(BF16), zeros], axis=2)


def _prepare_params(w_in, w_mem_kv, rpb, conv_w, conv_b, conv_ln_g, conv_ln_b, w_pa, w_pb, w_pc, w_o,
                    ln1_g, ln1_b, w_ffn_in, w_ffn_out, ln2_g, ln2_b):
    depth, heads = rpb.shape[:2]

    def row(v):
        return v.reshape(depth, 1, v.shape[1])

    return dict(
        w_in=_regroup_w_in(w_in),
        w_mem_kv=w_mem_kv.astype(BF16),
        na_bias=_na_bias_table(rpb.reshape((depth * heads,) + rpb.shape[2:])),
        conv_w=conv_w, conv_b=row(conv_b), conv_ln_g=row(conv_ln_g), conv_ln_b=row(conv_ln_b),
        w_pa=w_pa.astype(BF16), w_pb=w_pb.astype(BF16), w_pc=w_pc.astype(BF16),
        w_o=w_o.astype(BF16), ln1_g=row(ln1_g), ln1_b=row(ln1_b),
        w_ffn_in=w_ffn_in.astype(BF16), w_ffn_out=w_ffn_out.astype(BF16),
        ln2_g=row(ln2_g), ln2_b=row(ln2_b),
    )


def _encoder_layer(x, mem, b, t, p, layer, *, alpha):
    d = x.shape[1]
    gate_col = 0
    qkv_off = 3 * d
    q_col = qkv_off // V7X_LANES
    k_col = (qkv_off + NA_WIDTH) // V7X_LANES
    v_col = (qkv_off + 2 * NA_WIDTH) // V7X_LANES
    u_off = qkv_off + 3 * NA_WIDTH
    assert u_off % CONV_WIDTH == 0
    a_col = u_off // CONV_WIDTH
    qm_off = u_off + 2 * CONV_WIDTH
    assert qm_off % MEM_HEAD_DIM == 0 and qkv_off % V7X_LANES == 0
    qm_col = qm_off // MEM_HEAD_DIM

    h = _matmul(x, p["w_in"], layer, tm=IN_PROJ_TM, tn=IN_PROJ_TN, name="in_proj")
    h3 = h.reshape(b, t, h.shape[1])
    y_na = _na_attention(h3, p["na_bias"], layer, q_col=q_col, k_col=k_col, v_col=v_col)
    mt = mem.shape[1]
    kv = _matmul(mem.reshape(b * mt, d), p["w_mem_kv"], layer, tm=b * mt, tn=MEM_KV_TN, name="mem_kv")
    y_mem = _mem_attention(h3, kv.reshape(b, mt, 2 * MEM_WIDTH), q_col=qm_col)
    mixed = _conv_mix(h, y_na.reshape(b * t, NA_WIDTH), y_mem.reshape(b * t, MEM_WIDTH),
                      p["conv_w"], p["conv_b"], p["conv_ln_g"], p["conv_ln_b"],
                      p["w_pa"], p["w_pb"], p["w_pc"], layer,
                      t=t, a_col=a_col, g_col=a_col + 1, gate_col=gate_col)
    x = _proj_ln(mixed, x, p["w_o"], p["ln1_g"], p["ln1_b"], layer, alpha=alpha)
    return _ffn(x, p["w_ffn_in"], p["w_ffn_out"], p["ln2_g"], p["ln2_b"], layer, alpha=alpha)


def _run_trunk(x, mem, p, depth, alpha):
    b, t, d = x.shape
    y = x.reshape(b * t, d)
    for layer in range(depth):
        y = _encoder_layer(y, mem, b, t, p, layer, alpha=alpha)
    return y.reshape(b, t, d)


def kernel(x_prompt, x_sample, mem_prompt, mem_sample, w_in, w_mem_kv, rpb, conv_w, conv_b, conv_ln_g,
           conv_ln_b, w_pa, w_pb, w_pc, w_o, ln1_g, ln1_b, w_ffn_in, w_ffn_out, ln2_g, ln2_b):
    depth = w_in.shape[0]
    alpha = (2 * depth) ** 0.25
    p = _prepare_params(w_in, w_mem_kv, rpb, conv_w, conv_b, conv_ln_g, conv_ln_b, w_pa, w_pb, w_pc, w_o,
                        ln1_g, ln1_b, w_ffn_in, w_ffn_out, ln2_g, ln2_b)
    y_prompt = _run_trunk(x_prompt, mem_prompt, p, depth, alpha)
    y_sample = _run_trunk(x_sample, mem_sample, p, depth, alpha)
    return (y_prompt, y_sample)
```
